```python
import math
import jax, jax.numpy as jnp
from jax import lax
import numpy as np

D_MODEL = 1024
BATCH = 8
SEQ = 4096
DEPTH = 1

D_MIX = D_MODEL
D_RNN = D_MIX // 2
RNN_BLOCKS = 8
RNN_BW = D_RNN // RNN_BLOCKS
CONV_W = 4
RG_C = 8.0
D_ATT = D_MIX - D_RNN
HEAD_DIM = 64
N_HEADS = D_ATT // HEAD_DIM
Q_BLOCK = 128
D_FF = 2816
N_IN = 2 * D_RNN + 3 * D_ATT
EPS = 1e-6

kernel_name = "hybrid_rglru_stickbreaking_macaron"


def _rms_norm(x, g):
    xf = x.astype(jnp.float32)
    r = lax.rsqrt(jnp.mean(xf * xf, axis=-1, keepdims=True) + EPS)
    return (xf * r * g.astype(jnp.float32)).astype(x.dtype)


def _swiglu(x, w_gate, w_up, w_down):
    return (jax.nn.silu(x @ w_gate) * (x @ w_up)) @ w_down


def _lin_combine(left, right):
    a1, b1 = left
    a2, b2 = right
    return a1 * a2, a2 * b1 + b2


def _rg_lru_group(xr, gate, conv_w, conv_b, w_a, b_a, w_x, b_x, lam):
    bsz, seq, _ = xr.shape
    kern = conv_w.astype(xr.dtype)[:, None, :]
    xc = lax.conv_general_dilated(
        xr, kern, window_strides=(1,), padding=[(CONV_W - 1, 0)],
        dimension_numbers=("NWC", "WIO", "NWC"), feature_group_count=D_RNN,
    ) + conv_b
    xb = xc.reshape(bsz, seq, RNN_BLOCKS, RNN_BW)
    r = jax.nn.sigmoid(jnp.einsum("bsnc,ncd->bsnd", xb, w_a).reshape(bsz, seq, D_RNN) + b_a)
    i = jax.nn.sigmoid(jnp.einsum("bsnc,ncd->bsnd", xb, w_x).reshape(bsz, seq, D_RNN) + b_x)
    log_a = RG_C * r.astype(jnp.float32) * jax.nn.log_sigmoid(lam.astype(jnp.float32))
    a = jnp.exp(log_a)
    mult = jnp.sqrt(-jnp.expm1(2.0 * log_a))
    b = mult * (i * xc).astype(jnp.float32)
    _, h = lax.associative_scan(_lin_combine, (a, b), axis=1)
    return h.astype(xr.dtype) * jax.nn.gelu(gate)


def _stick_breaking(q, k, v):
    seq = q.shape[2]
    k_pos = jnp.arange(seq)
    kf = k.astype(jnp.float32)
    vf = v.astype(jnp.float32)

    def block(i):
        start = i * Q_BLOCK
        qb = lax.dynamic_slice_in_dim(q, start, Q_BLOCK, axis=2).astype(jnp.float32)
        z = jnp.einsum("bhqd,bhkd->bhqk", qb, kf)
        q_pos = start + jnp.arange(Q_BLOCK)
        causal = k_pos[None, :] < q_pos[:, None]
        log_beta = jax.nn.log_sigmoid(z)
        log_1m = jnp.where(causal, jax.nn.log_sigmoid(-z), 0.0)
        tail = lax.cumsum(log_1m, axis=log_1m.ndim - 1, reverse=True) - log_1m
        w = jnp.where(causal, jnp.exp(log_beta + tail), 0.0)
        return jnp.einsum("bhqk,bhkd->bhqd", w, vf)

    out = lax.map(block, jnp.arange(seq // Q_BLOCK))
    nb, bsz, nh, qb, dh = out.shape
    out = jnp.transpose(out, (1, 0, 3, 2, 4)).reshape(bsz, seq, nh * dh)
    return out.astype(q.dtype)


def _fwd_setup_inputs(seed: int = 0) -> dict:
    key = jax.random.key(seed)
    ks = jax.random.split(key, 24)
    f32 = jnp.float32

    def nrm(k, shape, scale):
        return jax.random.normal(k, shape, f32) * scale

    def gain(k, n):
        return 1.0 + 0.02 * jax.random.normal(k, (DEPTH, n), f32)

    a_base = jax.random.uniform(ks[13], (DEPTH, D_RNN), f32, 0.9, 0.999)
    s = a_base ** (1.0 / RG_C)
    rg_lambda = jnp.log(s) - jnp.log1p(-s)
    return {
        "x": nrm(ks[0], (BATCH, SEQ, D_MODEL), 1.0),
        "ffn1_norm": gain(ks[1], D_MODEL),
        "ffn1_w_gate": nrm(ks[2], (DEPTH, D_MODEL, D_FF), D_MODEL ** -0.5),
        "ffn1_w_up": nrm(ks[3], (DEPTH, D_MODEL, D_FF), D_MODEL ** -0.5),
        "ffn1_w_down": nrm(ks[4], (DEPTH, D_FF, D_MODEL), D_FF ** -0.5),
        "mix_norm": gain(ks[5], D_MODEL),
        "w_in": nrm(ks[6], (DEPTH, D_MODEL, N_IN), D_MODEL ** -0.5),
        "conv_w": nrm(ks[7], (DEPTH, CONV_W, D_RNN), CONV_W ** -0.5),
        "conv_b": nrm(ks[8], (DEPTH, D_RNN), 0.01),
        "rg_w_a": nrm(ks[9], (DEPTH, RNN_BLOCKS, RNN_BW, RNN_BW), RNN_BW ** -0.5),
        "rg_b_a": nrm(ks[10], (DEPTH, D_RNN), 0.01),
        "rg_w_x": nrm(ks[11], (DEPTH, RNN_BLOCKS, RNN_BW, RNN_BW), RNN_BW ** -0.5),
        "rg_b_x": nrm(ks[12], (DEPTH, D_RNN), 0.01),
        "rg_lambda": rg_lambda,
        "q_norm": gain(ks[14], HEAD_DIM),
        "k_norm": gain(ks[15], HEAD_DIM),
        "rnn_out_norm": gain(ks[16], D_RNN),
        "attn_out_norm": gain(ks[17], D_ATT),
        "w_out": nrm(ks[18], (DEPTH, D_MIX, D_MODEL), D_MIX ** -0.5),
        "ffn2_norm": gain(ks[19], D_MODEL),
        "ffn2_w_gate": nrm(ks[20], (DEPTH, D_MODEL, D_FF), D_MODEL ** -0.5),
        "ffn2_w_up": nrm(ks[21], (DEPTH, D_MODEL, D_FF), D_MODEL ** -0.5),
        "ffn2_w_down": nrm(ks[22], (DEPTH, D_FF, D_MODEL), D_FF ** -0.5),
    }


def _fwd_reference(x, ffn1_norm, ffn1_w_gate, ffn1_w_up, ffn1_w_down, mix_norm, w_in,
              conv_w, conv_b, rg_w_a, rg_b_a, rg_w_x, rg_b_x, rg_lambda,
              q_norm, k_norm, rnn_out_norm, attn_out_norm, w_out,
              ffn2_norm, ffn2_w_gate, ffn2_w_up, ffn2_w_down):
    bsz, seq, _ = x.shape
    scale = 1.0 / math.sqrt(HEAD_DIM)
    for l in range(DEPTH):
        x = x + 0.5 * _swiglu(_rms_norm(x, ffn1_norm[l]), ffn1_w_gate[l], ffn1_w_up[l], ffn1_w_down[l])

        h = _rms_norm(x, mix_norm[l])
        proj = h @ w_in[l]
        xr, gate, q, k, v = jnp.split(
            proj, [D_RNN, 2 * D_RNN, 2 * D_RNN + D_ATT, 2 * D_RNN + 2 * D_ATT], axis=-1)

        y_rnn = _rg_lru_group(xr, gate, conv_w[l], conv_b[l], rg_w_a[l], rg_b_a[l],
                              rg_w_x[l], rg_b_x[l], rg_lambda[l])

        def heads(t):
            return jnp.transpose(t.reshape(bsz, seq, N_HEADS, HEAD_DIM), (0, 2, 1, 3))
        qh = _rms_norm(heads(q), q_norm[l]) * scale
        kh = _rms_norm(heads(k), k_norm[l])
        y_att = _stick_breaking(qh, kh, heads(v))

        y = jnp.concatenate([_rms_norm(y_rnn, rnn_out_norm[l]),
                             _rms_norm(y_att, attn_out_norm[l])], axis=-1)
        x = x + y @ w_out[l]

        x = x + 0.5 * _swiglu(_rms_norm(x, ffn2_norm[l]), ffn2_w_gate[l], ffn2_w_up[l], ffn2_w_down[l])
    return x


import jax as _jax
import jax.numpy as _jnp

TWIN_FORMAT = 'train_step'
FWD_PARAMS = ['x', 'ffn1_norm', 'ffn1_w_gate', 'ffn1_w_up', 'ffn1_w_down', 'mix_norm', 'w_in', 'conv_w', 'conv_b', 'rg_w_a', 'rg_b_a', 'rg_w_x', 'rg_b_x', 'rg_lambda', 'q_norm', 'k_norm', 'rnn_out_norm', 'attn_out_norm', 'w_out', 'ffn2_norm', 'ffn2_w_gate', 'ffn2_w_up', 'ffn2_w_down']
TWIN_WEIGHTS = ['ffn1_norm', 'ffn1_w_gate', 'ffn1_w_up', 'ffn1_w_down', 'mix_norm', 'w_in', 'conv_w', 'conv_b', 'rg_w_a', 'rg_b_a', 'rg_w_x', 'rg_b_x', 'rg_lambda', 'q_norm', 'k_norm', 'rnn_out_norm', 'attn_out_norm', 'w_out', 'ffn2_norm', 'ffn2_w_gate', 'ffn2_w_up', 'ffn2_w_down']
TWIN_DIFF_INPUT = 'x'
TWIN_INPUTS = ['x', 'ffn1_norm', 'ffn1_w_gate', 'ffn1_w_up', 'ffn1_w_down', 'mix_norm', 'w_in', 'conv_w', 'conv_b', 'rg_w_a', 'rg_b_a', 'rg_w_x', 'rg_b_x', 'rg_lambda', 'q_norm', 'k_norm', 'rnn_out_norm', 'attn_out_norm', 'w_out', 'ffn2_norm', 'ffn2_w_gate', 'ffn2_w_up', 'ffn2_w_down', 'loss_target', 'm_ffn1_norm', 'm_ffn1_w_gate', 'm_ffn1_w_up', 'm_ffn1_w_down', 'm_mix_norm', 'm_w_in', 'm_conv_w', 'm_conv_b', 'm_rg_w_a', 'm_rg_b_a', 'm_rg_w_x', 'm_rg_b_x', 'm_rg_lambda', 'm_q_norm', 'm_k_norm', 'm_rnn_out_norm', 'm_attn_out_norm', 'm_w_out', 'm_ffn2_norm', 'm_ffn2_w_gate', 'm_ffn2_w_up', 'm_ffn2_w_down', 'v_ffn1_norm', 'v_ffn1_w_gate', 'v_ffn1_w_up', 'v_ffn1_w_down', 'v_mix_norm', 'v_w_in', 'v_conv_w', 'v_conv_b', 'v_rg_w_a', 'v_rg_b_a', 'v_rg_w_x', 'v_rg_b_x', 'v_rg_lambda', 'v_q_norm', 'v_k_norm', 'v_rnn_out_norm', 'v_attn_out_norm', 'v_w_out', 'v_ffn2_norm', 'v_ffn2_w_gate', 'v_ffn2_w_up', 'v_ffn2_w_down']
TWIN_OUTPUTS = ['loss', 'grad_x', 'grad_ffn1_norm', 'grad_ffn1_w_gate', 'grad_ffn1_w_up', 'grad_ffn1_w_down', 'grad_mix_norm', 'grad_w_in', 'grad_conv_w', 'grad_conv_b', 'grad_rg_w_a', 'grad_rg_b_a', 'grad_rg_w_x', 'grad_rg_b_x', 'grad_rg_lambda', 'grad_q_norm', 'grad_k_norm', 'grad_rnn_out_norm', 'grad_attn_out_norm', 'grad_w_out', 'grad_ffn2_norm', 'grad_ffn2_w_gate', 'grad_ffn2_w_up', 'grad_ffn2_w_down', 'delta_ffn1_norm', 'delta_ffn1_w_gate', 'delta_ffn1_w_up', 'delta_ffn1_w_down', 'delta_mix_norm', 'delta_w_in', 'delta_conv_w', 'delta_conv_b', 'delta_rg_w_a', 'delta_rg_b_a', 'delta_rg_w_x', 'delta_rg_b_x', 'delta_rg_lambda', 'delta_q_norm', 'delta_k_norm', 'delta_rnn_out_norm', 'delta_attn_out_norm', 'delta_w_out', 'delta_ffn2_norm', 'delta_ffn2_w_gate', 'delta_ffn2_w_up', 'delta_ffn2_w_down', 'new_m_ffn1_norm', 'new_m_ffn1_w_gate', 'new_m_ffn1_w_up', 'new_m_ffn1_w_down', 'new_m_mix_norm', 'new_m_w_in', 'new_m_conv_w', 'new_m_conv_b', 'new_m_rg_w_a', 'new_m_rg_b_a', 'new_m_rg_w_x', 'new_m_rg_b_x', 'new_m_rg_lambda', 'new_m_q_norm', 'new_m_k_norm', 'new_m_rnn_out_norm', 'new_m_attn_out_norm', 'new_m_w_out', 'new_m_ffn2_norm', 'new_m_ffn2_w_gate', 'new_m_ffn2_w_up', 'new_m_ffn2_w_down', 'new_v_ffn1_norm', 'new_v_ffn1_w_gate', 'new_v_ffn1_w_up', 'new_v_ffn1_w_down', 'new_v_mix_norm', 'new_v_w_in', 'new_v_conv_w', 'new_v_conv_b', 'new_v_rg_w_a', 'new_v_rg_b_a', 'new_v_rg_w_x', 'new_v_rg_b_x', 'new_v_rg_lambda', 'new_v_q_norm', 'new_v_k_norm', 'new_v_rnn_out_norm', 'new_v_attn_out_norm', 'new_v_w_out', 'new_v_ffn2_norm', 'new_v_ffn2_w_gate', 'new_v_ffn2_w_up', 'new_v_ffn2_w_down']
TWIN_LEAF_KINDS = {'loss': 'loss', 'grad_x': 'grad_x', 'grad_ffn1_norm': 'grad_w', 'grad_ffn1_w_gate': 'grad_w', 'grad_ffn1_w_up': 'grad_w', 'grad_ffn1_w_down': 'grad_w', 'grad_mix_norm': 'grad_w', 'grad_w_in': 'grad_w', 'grad_conv_w': 'grad_w', 'grad_conv_b': 'grad_w', 'grad_rg_w_a': 'grad_w', 'grad_rg_b_a': 'grad_w', 'grad_rg_w_x': 'grad_w', 'grad_rg_b_x': 'grad_w', 'grad_rg_lambda': 'grad_w', 'grad_q_norm': 'grad_w', 'grad_k_norm': 'grad_w', 'grad_rnn_out_norm': 'grad_w', 'grad_attn_out_norm': 'grad_w', 'grad_w_out': 'grad_w', 'grad_ffn2_norm': 'grad_w', 'grad_ffn2_w_gate': 'grad_w', 'grad_ffn2_w_up': 'grad_w', 'grad_ffn2_w_down': 'grad_w', 'delta_ffn1_norm': 'delta_w', 'delta_ffn1_w_gate': 'delta_w', 'delta_ffn1_w_up': 'delta_w', 'delta_ffn1_w_down': 'delta_w', 'delta_mix_norm': 'delta_w', 'delta_w_in': 'delta_w', 'delta_conv_w': 'delta_w', 'delta_conv_b': 'delta_w', 'delta_rg_w_a': 'delta_w', 'delta_rg_b_a': 'delta_w', 'delta_rg_w_x': 'delta_w', 'delta_rg_b_x': 'delta_w', 'delta_rg_lambda': 'delta_w', 'delta_q_norm': 'delta_w', 'delta_k_norm': 'delta_w', 'delta_rnn_out_norm': 'delta_w', 'delta_attn_out_norm': 'delta_w', 'delta_w_out': 'delta_w', 'delta_ffn2_norm': 'delta_w', 'delta_ffn2_w_gate': 'delta_w', 'delta_ffn2_w_up': 'delta_w', 'delta_ffn2_w_down': 'delta_w', 'new_m_ffn1_norm': 'new_m', 'new_m_ffn1_w_gate': 'new_m', 'new_m_ffn1_w_up': 'new_m', 'new_m_ffn1_w_down': 'new_m', 'new_m_mix_norm': 'new_m', 'new_m_w_in': 'new_m', 'new_m_conv_w': 'new_m', 'new_m_conv_b': 'new_m', 'new_m_rg_w_a': 'new_m', 'new_m_rg_b_a': 'new_m', 'new_m_rg_w_x': 'new_m', 'new_m_rg_b_x': 'new_m', 'new_m_rg_lambda': 'new_m', 'new_m_q_norm': 'new_m', 'new_m_k_norm': 'new_m', 'new_m_rnn_out_norm': 'new_m', 'new_m_attn_out_norm': 'new_m', 'new_m_w_out': 'new_m', 'new_m_ffn2_norm': 'new_m', 'new_m_ffn2_w_gate': 'new_m', 'new_m_ffn2_w_up': 'new_m', 'new_m_ffn2_w_down': 'new_m', 'new_v_ffn1_norm': 'new_v', 'new_v_ffn1_w_gate': 'new_v', 'new_v_ffn1_w_up': 'new_v', 'new_v_ffn1_w_down': 'new_v', 'new_v_mix_norm': 'new_v', 'new_v_w_in': 'new_v', 'new_v_conv_w': 'new_v', 'new_v_conv_b': 'new_v', 'new_v_rg_w_a': 'new_v', 'new_v_rg_b_a': 'new_v', 'new_v_rg_w_x': 'new_v', 'new_v_rg_b_x': 'new_v', 'new_v_rg_lambda': 'new_v', 'new_v_q_norm': 'new_v', 'new_v_k_norm': 'new_v', 'new_v_rnn_out_norm': 'new_v', 'new_v_attn_out_norm': 'new_v', 'new_v_w_out': 'new_v', 'new_v_ffn2_norm': 'new_v', 'new_v_ffn2_w_gate': 'new_v', 'new_v_ffn2_w_up': 'new_v', 'new_v_ffn2_w_down': 'new_v'}


def _forward(args):
    return _fwd_reference(*[args[k] for k in FWD_PARAMS])


def _output_shape():
    def fwd():
        inp = _fwd_setup_inputs(0)
        return _fwd_reference(*[inp[k] for k in FWD_PARAMS])
    out = _jax.eval_shape(fwd)
    return out.shape, out.dtype

N_MICROBATCH = 1
ADAM_LR = 0.001
ADAM_B1 = 0.9
ADAM_B2 = 0.999
ADAM_EPS = 1e-08
ADAM_WD = 0.01
ADAM_STEP = 10
PER_EXAMPLE_BATCH_AXIS = {'x': 0, 'loss_target': 0}
SHARED_INPUTS = []
_WEIGHT_DTYPES = {'ffn1_norm': _jnp.float32, 'ffn1_w_gate': _jnp.float32, 'ffn1_w_up': _jnp.float32, 'ffn1_w_down': _jnp.float32, 'mix_norm': _jnp.float32, 'w_in': _jnp.float32, 'conv_w': _jnp.float32, 'conv_b': _jnp.float32, 'rg_w_a': _jnp.float32, 'rg_b_a': _jnp.float32, 'rg_w_x': _jnp.float32, 'rg_b_x': _jnp.float32, 'rg_lambda': _jnp.float32, 'q_norm': _jnp.float32, 'k_norm': _jnp.float32, 'rnn_out_norm': _jnp.float32, 'attn_out_norm': _jnp.float32, 'w_out': _jnp.float32, 'ffn2_norm': _jnp.float32, 'ffn2_w_gate': _jnp.float32, 'ffn2_w_up': _jnp.float32, 'ffn2_w_down': _jnp.float32}
MOMENT_SCALE = {'ffn1_norm': 6.187867e+00, 'ffn1_w_gate': 1.259016e-01, 'ffn1_w_up': 1.342533e-01, 'ffn1_w_down': 2.233405e-01, 'mix_norm': 7.969085e-01, 'w_in': 4.221392e-01, 'conv_w': 2.227878e+00, 'conv_b': 1.730526e+01, 'rg_w_a': 7.205715e-01, 'rg_b_a': 4.528127e-01, 'rg_w_x': 1.286706e+00, 'rg_b_x': 7.228329e-01, 'rg_lambda': 5.853905e-01, 'q_norm': 4.412976e-01, 'k_norm': 4.448132e-01, 'rnn_out_norm': 4.967979e+01, 'attn_out_norm': 3.199186e+01, 'w_out': 2.098822e+00, 'ffn2_norm': 6.153290e+00, 'ffn2_w_gate': 1.229172e-01, 'ffn2_w_up': 1.358153e-01, 'ffn2_w_down': 2.187454e-01}


def _to_microbatches(a, axis):
    t = _jnp.moveaxis(a, axis, 0)
    t = t.reshape((N_MICROBATCH, t.shape[0] // N_MICROBATCH) + t.shape[1:])
    return _jnp.moveaxis(t, 1, axis + 1)


def setup_inputs(seed: int = 0) -> dict:
    inp = _fwd_setup_inputs(seed)
    key = _jax.random.fold_in(_jax.random.key(seed), 7919)
    shape, _ = _output_shape()
    out = dict(inp)
    out["loss_target"] = _jax.random.normal(_jax.random.fold_in(key, 0), shape, _jnp.float32)
    for i, name in enumerate(TWIN_WEIGHTS):
        w = inp[name].astype(_jnp.float32)
        if MOMENT_SCALE is None:
            s = _jnp.sqrt(_jnp.mean(_jnp.square(w)) + 1e-30)
        else:
            s = MOMENT_SCALE[name]
        km, kv = _jax.random.split(_jax.random.fold_in(key, i + 1))
        out[name] = w
        out["m_" + name] = s * _jax.random.normal(km, w.shape, _jnp.float32)
        out["v_" + name] = (s * s) * _jax.random.uniform(kv, w.shape, _jnp.float32, 0.5, 1.5)
    if N_MICROBATCH > 1:
        for name, axis in PER_EXAMPLE_BATCH_AXIS.items():
            out[name] = _to_microbatches(out[name], axis)
    return {'x': out['x'], 'ffn1_norm': out['ffn1_norm'], 'ffn1_w_gate': out['ffn1_w_gate'], 'ffn1_w_up': out['ffn1_w_up'], 'ffn1_w_down': out['ffn1_w_down'], 'mix_norm': out['mix_norm'], 'w_in': out['w_in'], 'conv_w': out['conv_w'], 'conv_b': out['conv_b'], 'rg_w_a': out['rg_w_a'], 'rg_b_a': out['rg_b_a'], 'rg_w_x': out['rg_w_x'], 'rg_b_x': out['rg_b_x'], 'rg_lambda': out['rg_lambda'], 'q_norm': out['q_norm'], 'k_norm': out['k_norm'], 'rnn_out_norm': out['rnn_out_norm'], 'attn_out_norm': out['attn_out_norm'], 'w_out': out['w_out'], 'ffn2_norm': out['ffn2_norm'], 'ffn2_w_gate': out['ffn2_w_gate'], 'ffn2_w_up': out['ffn2_w_up'], 'ffn2_w_down': out['ffn2_w_down'], 'loss_target': out['loss_target'], 'm_ffn1_norm': out['m_ffn1_norm'], 'm_ffn1_w_gate': out['m_ffn1_w_gate'], 'm_ffn1_w_up': out['m_ffn1_w_up'], 'm_ffn1_w_down': out['m_ffn1_w_down'], 'm_mix_norm': out['m_mix_norm'], 'm_w_in': out['m_w_in'], 'm_conv_w': out['m_conv_w'], 'm_conv_b': out['m_conv_b'], 'm_rg_w_a': out['m_rg_w_a'], 'm_rg_b_a': out['m_rg_b_a'], 'm_rg_w_x': out['m_rg_w_x'], 'm_rg_b_x': out['m_rg_b_x'], 'm_rg_lambda': out['m_rg_lambda'], 'm_q_norm': out['m_q_norm'], 'm_k_norm': out['m_k_norm'], 'm_rnn_out_norm': out['m_rnn_out_norm'], 'm_attn_out_norm': out['m_attn_out_norm'], 'm_w_out': out['m_w_out'], 'm_ffn2_norm': out['m_ffn2_norm'], 'm_ffn2_w_gate': out['m_ffn2_w_gate'], 'm_ffn2_w_up': out['m_ffn2_w_up'], 'm_ffn2_w_down': out['m_ffn2_w_down'], 'v_ffn1_norm': out['v_ffn1_norm'], 'v_ffn1_w_gate': out['v_ffn1_w_gate'], 'v_ffn1_w_up': out['v_ffn1_w_up'], 'v_ffn1_w_down': out['v_ffn1_w_down'], 'v_mix_norm': out['v_mix_norm'], 'v_w_in': out['v_w_in'], 'v_conv_w': out['v_conv_w'], 'v_conv_b': out['v_conv_b'], 'v_rg_w_a': out['v_rg_w_a'], 'v_rg_b_a': out['v_rg_b_a'], 'v_rg_w_x': out['v_rg_w_x'], 'v_rg_b_x': out['v_rg_b_x'], 'v_rg_lambda': out['v_rg_lambda'], 'v_q_norm': out['v_q_norm'], 'v_k_norm': out['v_k_norm'], 'v_rnn_out_norm': out['v_rnn_out_norm'], 'v_attn_out_norm': out['v_attn_out_norm'], 'v_w_out': out['v_w_out'], 'v_ffn2_norm': out['v_ffn2_norm'], 'v_ffn2_w_gate': out['v_ffn2_w_gate'], 'v_ffn2_w_up': out['v_ffn2_w_up'], 'v_ffn2_w_down': out['v_ffn2_w_down']}


def _loss(weights, diff, rest, loss_target):
    with _jax.named_scope("forward"):
        args = {**rest, TWIN_DIFF_INPUT: diff, **{k: w.astype(_WEIGHT_DTYPES[k]) for k, w in weights.items()}}
        y = _forward(args)
    with _jax.named_scope("loss_head"):
        err = _jnp.square(y.astype(_jnp.float32) - loss_target)
        return 0.5 * _jnp.sum(_jnp.mean(err, axis=-1)) if err.ndim else 0.5 * err


def _adamw(w, g, m, v):
    m = ADAM_B1 * m + (1.0 - ADAM_B1) * g
    v = ADAM_B2 * v + (1.0 - ADAM_B2) * _jnp.square(g)
    m_hat = m / (1.0 - ADAM_B1 ** ADAM_STEP)
    v_hat = v / (1.0 - ADAM_B2 ** ADAM_STEP)
    delta = -ADAM_LR * (m_hat / (_jnp.sqrt(v_hat) + ADAM_EPS) + ADAM_WD * w)
    return delta, m, v


def reference(x, ffn1_norm, ffn1_w_gate, ffn1_w_up, ffn1_w_down, mix_norm, w_in, conv_w, conv_b, rg_w_a, rg_b_a, rg_w_x, rg_b_x, rg_lambda, q_norm, k_norm, rnn_out_norm, attn_out_norm, w_out, ffn2_norm, ffn2_w_gate, ffn2_w_up, ffn2_w_down, loss_target, m_ffn1_norm, m_ffn1_w_gate, m_ffn1_w_up, m_ffn1_w_down, m_mix_norm, m_w_in, m_conv_w, m_conv_b, m_rg_w_a, m_rg_b_a, m_rg_w_x, m_rg_b_x, m_rg_lambda, m_q_norm, m_k_norm, m_rnn_out_norm, m_attn_out_norm, m_w_out, m_ffn2_norm, m_ffn2_w_gate, m_ffn2_w_up, m_ffn2_w_down, v_ffn1_norm, v_ffn1_w_gate, v_ffn1_w_up, v_ffn1_w_down, v_mix_norm, v_w_in, v_conv_w, v_conv_b, v_rg_w_a, v_rg_b_a, v_rg_w_x, v_rg_b_x, v_rg_lambda, v_q_norm, v_k_norm, v_rnn_out_norm, v_attn_out_norm, v_w_out, v_ffn2_norm, v_ffn2_w_gate, v_ffn2_w_up, v_ffn2_w_down):
    given = dict(x=x, ffn1_norm=ffn1_norm, ffn1_w_gate=ffn1_w_gate, ffn1_w_up=ffn1_w_up, ffn1_w_down=ffn1_w_down, mix_norm=mix_norm, w_in=w_in, conv_w=conv_w, conv_b=conv_b, rg_w_a=rg_w_a, rg_b_a=rg_b_a, rg_w_x=rg_w_x, rg_b_x=rg_b_x, rg_lambda=rg_lambda, q_norm=q_norm, k_norm=k_norm, rnn_out_norm=rnn_out_norm, attn_out_norm=attn_out_norm, w_out=w_out, ffn2_norm=ffn2_norm, ffn2_w_gate=ffn2_w_gate, ffn2_w_up=ffn2_w_up, ffn2_w_down=ffn2_w_down, loss_target=loss_target, m_ffn1_norm=m_ffn1_norm, m_ffn1_w_gate=m_ffn1_w_gate, m_ffn1_w_up=m_ffn1_w_up, m_ffn1_w_down=m_ffn1_w_down, m_mix_norm=m_mix_norm, m_w_in=m_w_in, m_conv_w=m_conv_w, m_conv_b=m_conv_b, m_rg_w_a=m_rg_w_a, m_rg_b_a=m_rg_b_a, m_rg_w_x=m_rg_w_x, m_rg_b_x=m_rg_b_x, m_rg_lambda=m_rg_lambda, m_q_norm=m_q_norm, m_k_norm=m_k_norm, m_rnn_out_norm=m_rnn_out_norm, m_attn_out_norm=m_attn_out_norm, m_w_out=m_w_out, m_ffn2_norm=m_ffn2_norm, m_ffn2_w_gate=m_ffn2_w_gate, m_ffn2_w_up=m_ffn2_w_up, m_ffn2_w_down=m_ffn2_w_down, v_ffn1_norm=v_ffn1_norm, v_ffn1_w_gate=v_ffn1_w_gate, v_ffn1_w_up=v_ffn1_w_up, v_ffn1_w_down=v_ffn1_w_down, v_mix_norm=v_mix_norm, v_w_in=v_w_in, v_conv_w=v_conv_w, v_conv_b=v_conv_b, v_rg_w_a=v_rg_w_a, v_rg_b_a=v_rg_b_a, v_rg_w_x=v_rg_w_x, v_rg_b_x=v_rg_b_x, v_rg_lambda=v_rg_lambda, v_q_norm=v_q_norm, v_k_norm=v_k_norm, v_rnn_out_norm=v_rnn_out_norm, v_attn_out_norm=v_attn_out_norm, v_w_out=v_w_out, v_ffn2_norm=v_ffn2_norm, v_ffn2_w_gate=v_ffn2_w_gate, v_ffn2_w_up=v_ffn2_w_up, v_ffn2_w_down=v_ffn2_w_down)
    weights = {n: given[n] for n in TWIN_WEIGHTS}
    shared = {n: given[n] for n in SHARED_INPUTS}
    per_example = {n: given[n] for n in ['x']}
    grad_fn = _jax.value_and_grad(_loss, argnums=(0, 1))

    def one_microbatch(ex, loss_target):
        ex = dict(ex)
        diff = ex.pop(TWIN_DIFF_INPUT)
        return grad_fn(weights, diff, {**shared, **ex}, loss_target)

    if N_MICROBATCH == 1:
        loss, (grad_w, grad_x) = one_microbatch(per_example, given["loss_target"])
    else:
        def body(carry, xs):
            loss_sum, grad_sum = carry
            l_k, (gw_k, gx_k) = one_microbatch(xs[0], xs[1])
            with _jax.named_scope("update"):
                return (loss_sum + l_k, _jax.tree.map(_jnp.add, grad_sum, gw_k)), gx_k

        init = (_jnp.zeros((), _jnp.float32), _jax.tree.map(_jnp.zeros_like, weights))
        (loss, grad_w), grad_x = _jax.lax.scan(body, init, (per_example, given["loss_target"]))
    with _jax.named_scope("update"):
        delta_w, new_m, new_v = {}, {}, {}
        for n in TWIN_WEIGHTS:
            delta_w[n], new_m[n], new_v[n] = _adamw(weights[n], grad_w[n], given["m_" + n], given["v_" + n])
    return (loss, grad_x, *[grad_w[n] for n in TWIN_WEIGHTS], *[delta_w[n] for n in TWIN_WEIGHTS],
            *[new_m[n] for n in TWIN_WEIGHTS], *[new_v[n] for n in TWIN_WEIGHTS])
```

```python
import functools
import math

import jax
import jax.numpy as jnp
from jax import lax
from jax.experimental import pallas as pl
from jax.experimental.pallas import tpu as pltpu

F32 = jnp.float32
BF16 = jnp.bfloat16

D_MODEL = 1024
D_FF = 2816
D_RNN = 512
D_ATT = 512
N_HEADS = 8
HEAD_DIM = 64
N_IN = 2 * D_RNN + 3 * D_ATT
CONV_W = 4
RG_C = 8.0
EPS = 1e-6
N_DEV = 8

ADAM_LR = 0.001
ADAM_B1 = 0.9
ADAM_B2 = 0.999
ADAM_EPS = 1e-08
ADAM_WD = 0.01
ADAM_STEP = 10

VMEM_LIMIT_BYTES = 56 * 1024 * 1024
LANES = 128
SUBLANES = 8

MESH = pl.DeviceIdType.MESH


def _cp(*sem):
    return pltpu.CompilerParams(dimension_semantics=sem, vmem_limit_bytes=VMEM_LIMIT_BYTES)


def _nn(a, b):
    return jnp.dot(a, b, preferred_element_type=F32)


def _nt(a, b):
    return lax.dot_general(a, b, (((1,), (1,)), ((), ())), preferred_element_type=F32)


def _tn(a, b):
    return lax.dot_general(a, b, (((0,), (0,)), ((), ())), preferred_element_type=F32)


def _rms_scale(x, width):
    return lax.rsqrt(jnp.sum(x * x, axis=-1, keepdims=True) * (1.0 / width) + EPS)


def _rms_bwd(dn, x, g, r, width):
    u = dn * g
    m = jnp.sum(u * x, axis=-1, keepdims=True) * (1.0 / width)
    return r * u - x * (r * r * r) * m


def _acc(ref, val, first):
    @pl.when(first)
    def _():
        ref[...] = val

    @pl.when(jnp.logical_not(first))
    def _():
        ref[...] += val


def _sigmoid(x):
    return jax.nn.sigmoid(x)


_GELU_C = math.sqrt(2.0 / math.pi)


def _gelu(x):
    return 0.5 * x * (1.0 + jnp.tanh(_GELU_C * (x + 0.044715 * (x * x * x))))


def _gelu_grad(x):
    t = jnp.tanh(_GELU_C * (x + 0.044715 * (x * x * x)))
    return 0.5 * (1.0 + t) + 0.5 * x * (1.0 - t * t) * (_GELU_C * (1.0 + 3.0 * 0.044715 * (x * x)))


def _log_sigmoid(x):
    return jnp.minimum(x, 0.0) - jnp.log(1.0 + jnp.exp(-jnp.abs(x)))


def _neg_expm1(x):
    series = x * (1.0 + x * (0.5 + x * (1.0 / 6.0 + x * (1.0 / 24.0))))
    return -jnp.where(x > -0.01, series, jnp.exp(x) - 1.0)


FFN_TF = 256


def _ffn_fwd(x, g, wgt, wut, wd, name):
    s = x.shape[0]
    tm = min(1024, s)
    nj = D_FF // FFN_TF

    def body(x_ref, g_ref, wg_ref, wu_ref, wd_ref, xo_ref, n_ref, gg_ref, uu_ref, acc_ref):
        j = pl.program_id(1)

        @pl.when(j == 0)
        def _():
            xv = x_ref[...]
            n_ref[...] = (xv * _rms_scale(xv, D_MODEL) * g_ref[...]).astype(BF16)
            acc_ref[...] = jnp.zeros_like(acc_ref)

        n = n_ref[...]
        gg = _nt(n, wg_ref[...])
        uu = _nt(n, wu_ref[...])
        gg_ref[...] = gg.astype(BF16)
        uu_ref[...] = uu.astype(BF16)
        act = (gg * _sigmoid(gg) * uu).astype(BF16)
        acc_ref[...] += _nn(act, wd_ref[...])

        @pl.when(j == nj - 1)
        def _():
            xo_ref[...] = x_ref[...] + 0.5 * acc_ref[...]

    row = lambda i, j: (i, 0)
    wrow = lambda i, j: (j, 0)
    return pl.pallas_call(
        body,
        name=name,
        grid=(s // tm, nj),
        in_specs=[
            pl.BlockSpec((tm, D_MODEL), row),
            pl.BlockSpec((1, D_MODEL), lambda i, j: (0, 0)),
            pl.BlockSpec((FFN_TF, D_MODEL), wrow),
            pl.BlockSpec((FFN_TF, D_MODEL), wrow),
            pl.BlockSpec((FFN_TF, D_MODEL), wrow),
        ],
        out_specs=[
            pl.BlockSpec((tm, D_MODEL), row),
            pl.BlockSpec((tm, D_MODEL), row),
            pl.BlockSpec((tm, FFN_TF), lambda i, j: (i, j)),
            pl.BlockSpec((tm, FFN_TF), lambda i, j: (i, j)),
        ],
        out_shape=[
            jax.ShapeDtypeStruct((s, D_MODEL), F32),
            jax.ShapeDtypeStruct((s, D_MODEL), BF16),
            jax.ShapeDtypeStruct((s, D_FF), BF16),
            jax.ShapeDtypeStruct((s, D_FF), BF16),
        ],
        scratch_shapes=[pltpu.VMEM((tm, D_MODEL), F32)],
        compiler_params=_cp("arbitrary", "arbitrary"),
    )(x, g, wgt, wut, wd)


def _ffn_bwd(dyb, dy, x, g, n, gg, uu, wgt, wut, wd, name):
    s = x.shape[0]
    tm = min(512, s)
    ni = s // tm
    nj = D_FF // FFN_TF

    def body(dyb_ref, dy_ref, x_ref, g_ref, n_ref, gg_ref, uu_ref, wg_ref, wu_ref, wd_ref,
             dx_ref, dxb_ref, dwg_ref, dwu_ref, dwd_ref, dg_ref, dn_ref):
        j = pl.program_id(0)
        i = pl.program_id(1)
        rows = pl.ds(pl.multiple_of(i * tm, tm), tm)
        dyv = dyb_ref[...]
        da = 0.5 * _nt(dyv, wd_ref[...])
        gv = gg_ref[...].astype(F32)
        uv = uu_ref[...].astype(F32)
        sg = _sigmoid(gv)
        si = gv * sg
        dgg = (da * uv * (sg * (1.0 + gv * (1.0 - sg)))).astype(BF16)
        duu = (da * si).astype(BF16)
        act = (si * uv).astype(BF16)
        nv = n_ref[...]
        first_i = i == 0
        _acc(dwd_ref, 0.5 * _tn(act, dyv), first_i)
        _acc(dwg_ref, _tn(dgg, nv), first_i)
        _acc(dwu_ref, _tn(duu, nv), first_i)
        cn = _nn(dgg, wg_ref[...]) + _nn(duu, wu_ref[...])

        @pl.when(j == 0)
        def _():
            dn_ref[rows, :] = cn

        @pl.when(j > 0)
        def _():
            dn_ref[rows, :] += cn

        @pl.when(j == nj - 1)
        def _():
            dn = dn_ref[rows, :]
            xv = x_ref[...]
            r = _rms_scale(xv, D_MODEL)
            dx = dy_ref[...] + _rms_bwd(dn, xv, g_ref[...], r, D_MODEL)
            dx_ref[...] = dx
            dxb_ref[...] = dx.astype(BF16)
            _acc(dg_ref, jnp.sum(dn * xv * r, axis=0, keepdims=True), first_i)

    last = lambda j, i: (jnp.where(j == nj - 1, i, 0), 0)
    row = lambda j, i: (i, 0)
    wrow = lambda j, i: (j, 0)
    return pl.pallas_call(
        body,
        name=name,
        grid=(nj, ni),
        in_specs=[
            pl.BlockSpec((tm, D_MODEL), row),
            pl.BlockSpec((tm, D_MODEL), last),
            pl.BlockSpec((tm, D_MODEL), last),
            pl.BlockSpec((1, D_MODEL), lambda j, i: (0, 0)),
            pl.BlockSpec((tm, D_MODEL), row),
            pl.BlockSpec((tm, FFN_TF), lambda j, i: (i, j)),
            pl.BlockSpec((tm, FFN_TF), lambda j, i: (i, j)),
            pl.BlockSpec((FFN_TF, D_MODEL), wrow),
            pl.BlockSpec((FFN_TF, D_MODEL), wrow),
            pl.BlockSpec((FFN_TF, D_MODEL), wrow),
        ],
        out_specs=[
            pl.BlockSpec((tm, D_MODEL), last),
            pl.BlockSpec((tm, D_MODEL), last),
            pl.BlockSpec((FFN_TF, D_MODEL), wrow),
            pl.BlockSpec((FFN_TF, D_MODEL), wrow),
            pl.BlockSpec((FFN_TF, D_MODEL), wrow),
            pl.BlockSpec((1, D_MODEL), lambda j, i: (0, 0)),
        ],
        out_shape=[
            jax.ShapeDtypeStruct((s, D_MODEL), F32),
            jax.ShapeDtypeStruct((s, D_MODEL), BF16),
            jax.ShapeDtypeStruct((D_FF, D_MODEL), F32),
            jax.ShapeDtypeStruct((D_FF, D_MODEL), F32),
            jax.ShapeDtypeStruct((D_FF, D_MODEL), F32),
            jax.ShapeDtypeStruct((1, D_MODEL), F32),
        ],
        scratch_shapes=[pltpu.VMEM((s, D_MODEL), F32)],
        compiler_params=_cp("arbitrary", "arbitrary"),
    )(dyb, dy, x, g, n, gg, uu, wgt, wut, wd)


def _heads_spec(tm):
    return pl.BlockSpec((N_HEADS, tm, HEAD_DIM), lambda i: (0, i, 0))


def _in_fwd(x1, gm, wint):
    s = x1.shape[0]
    tm = min(512, s)

    def body(x_ref, g_ref, w_ref, h_ref, xr_ref, gate_ref, q_ref, k_ref, v_ref):
        xv = x_ref[...]
        h = (xv * _rms_scale(xv, D_MODEL) * g_ref[...]).astype(BF16)
        h_ref[...] = h
        xr_ref[...] = _nt(h, w_ref[0:D_RNN, :])
        gate_ref[...] = _nt(h, w_ref[D_RNN:2 * D_RNN, :])
        for c, ref in ((2, q_ref), (3, k_ref), (4, v_ref)):
            p = _nt(h, w_ref[512 * c:512 * c + 512, :])
            for hh in range(N_HEADS):
                ref[hh] = p[:, HEAD_DIM * hh:HEAD_DIM * (hh + 1)]

    row = lambda i: (i, 0)
    hs = jax.ShapeDtypeStruct((N_HEADS, s, HEAD_DIM), F32)
    return pl.pallas_call(
        body,
        name="mix_in_fwd",
        grid=(s // tm,),
        in_specs=[
            pl.BlockSpec((tm, D_MODEL), row),
            pl.BlockSpec((1, D_MODEL), lambda i: (0, 0)),
            pl.BlockSpec((N_IN, D_MODEL), lambda i: (0, 0)),
        ],
        out_specs=[
            pl.BlockSpec((tm, D_MODEL), row),
            pl.BlockSpec((tm, D_RNN), row),
            pl.BlockSpec((tm, D_RNN), row),
            _heads_spec(tm), _heads_spec(tm), _heads_spec(tm),
        ],
        out_shape=[
            jax.ShapeDtypeStruct((s, D_MODEL), BF16),
            jax.ShapeDtypeStruct((s, D_RNN), F32),
            jax.ShapeDtypeStruct((s, D_RNN), F32),
            hs, hs, hs,
        ],
        compiler_params=_cp("arbitrary"),
    )(x1, gm, wint)


def _in_bwd(dx2, x1, gm, h, wint, dxr, dgate, dq, dk, dv):
    s = x1.shape[0]
    tm = min(256, s)

    def body(dx2_ref, x_ref, g_ref, h_ref, w_ref, dxr_ref, dgate_ref, dq_ref, dk_ref, dv_ref,
             dx1_ref, dx1b_ref, dw_ref, dg_ref):
        first = pl.program_id(0) == 0
        hv = h_ref[...]
        heads = lambda ref: jnp.concatenate([ref[hh] for hh in range(N_HEADS)], axis=1)
        pieces = (dxr_ref[...], dgate_ref[...], heads(dq_ref), heads(dk_ref), heads(dv_ref))
        dh = jnp.zeros((tm, D_MODEL), F32)
        for c, p in enumerate(pieces):
            pb = p.astype(BF16)
            rows = slice(512 * c, 512 * c + 512)
            dh = dh + _nn(pb, w_ref[rows, :])
            contrib = _tn(pb, hv)

            @pl.when(first)
            def _():
                dw_ref[rows, :] = contrib

            @pl.when(jnp.logical_not(first))
            def _():
                dw_ref[rows, :] += contrib

        xv = x_ref[...]
        r = _rms_scale(xv, D_MODEL)
        dx = dx2_ref[...] + _rms_bwd(dh, xv, g_ref[...], r, D_MODEL)
        dx1_ref[...] = dx
        dx1b_ref[...] = dx.astype(BF16)
        _acc(dg_ref, jnp.sum(dh * xv * r, axis=0, keepdims=True), first)

    row = lambda i: (i, 0)
    const = lambda i: (0, 0)
    return pl.pallas_call(
        body,
        name="mix_in_bwd",
        grid=(s // tm,),
        in_specs=[
            pl.BlockSpec((tm, D_MODEL), row),
            pl.BlockSpec((tm, D_MODEL), row),
            pl.BlockSpec((1, D_MODEL), const),
            pl.BlockSpec((tm, D_MODEL), row),
            pl.BlockSpec((N_IN, D_MODEL), const),
            pl.BlockSpec((tm, D_RNN), row),
            pl.BlockSpec((tm, D_RNN), row),
            _heads_spec(tm), _heads_spec(tm), _heads_spec(tm),
        ],
        out_specs=[
            pl.BlockSpec((tm, D_MODEL), row),
            pl.BlockSpec((tm, D_MODEL), row),
            pl.BlockSpec((N_IN, D_MODEL), const),
            pl.BlockSpec((1, D_MODEL), const),
        ],
        out_shape=[
            jax.ShapeDtypeStruct((s, D_MODEL), F32),
            jax.ShapeDtypeStruct((s, D_MODEL), BF16),
            jax.ShapeDtypeStruct((N_IN, D_MODEL), F32),
            jax.ShapeDtypeStruct((1, D_MODEL), F32),
        ],
        compiler_params=_cp("arbitrary"),
    )(dx2, x1, gm, h, wint, dxr, dgate, dq, dk, dv)


RG_CHUNK = 512
HALO = SUBLANES


def _rg_gates(xc, wa_ref, ba_ref, wx_ref, bx_ref, lam_ref):
    xcb = xc.astype(BF16)
    r = _sigmoid(_nn(xcb, wa_ref[...]) + ba_ref[...])
    ig = _sigmoid(_nn(xcb, wx_ref[...]) + bx_ref[...])
    ls = _log_sigmoid(lam_ref[...])
    log_a = RG_C * r * ls
    a = jnp.exp(log_a)
    em = _neg_expm1(2.0 * log_a)
    return xcb, r, ig, ls, a, em


def _conv(ext_ref, cw_ref, cb_ref, tc):
    xc = cb_ref[...] + cw_ref[0:1, :] * ext_ref[pl.ds(HALO - 3, tc), :]
    for j in range(1, CONV_W):
        xc = xc + cw_ref[j:j + 1, :] * ext_ref[pl.ds(HALO - 3 + j, tc), :]
    return xc


def _rg_fwd(xr, gate, cw, cb, wa, ba, wx, bx, lam, gro):
    s = xr.shape[0]
    tc = min(RG_CHUNK, s)
    nsub = tc // SUBLANES

    def body(xr_ref, gate_ref, cw_ref, cb_ref, wa_ref, ba_ref, wx_ref, bx_ref, lam_ref, gro_ref,
             h_ref, yn_ref, ext_ref, a_ref, b_ref, hc_ref):
        @pl.when(pl.program_id(0) == 0)
        def _():
            ext_ref[0:HALO, :] = jnp.zeros((HALO, D_RNN), F32)
            hc_ref[...] = jnp.zeros_like(hc_ref)

        ext_ref[HALO:HALO + tc, :] = xr_ref[...]
        xc = _conv(ext_ref, cw_ref, cb_ref, tc)
        _, _, ig, _, a, em = _rg_gates(xc, wa_ref, ba_ref, wx_ref, bx_ref, lam_ref)
        a_ref[...] = a
        b_ref[...] = jnp.sqrt(em) * (ig * xc)
        row = lax.broadcasted_iota(jnp.int32, (SUBLANES, D_RNN), 0)

        def step(k, hprev):
            rows = pl.ds(pl.multiple_of(k * SUBLANES, SUBLANES), SUBLANES)
            av = a_ref[rows, :]
            bv = b_ref[rows, :]
            for d in (1, 2, 4):
                a_sh = jnp.where(row >= d, pltpu.roll(av, d, 0), 1.0)
                b_sh = jnp.where(row >= d, pltpu.roll(bv, d, 0), 0.0)
                bv = av * b_sh + bv
                av = av * a_sh
            hv = av * hprev + bv
            h_ref[rows, :] = hv
            return hv[SUBLANES - 1:SUBLANES, :]

        hlast = lax.fori_loop(0, nsub, step, hc_ref[0:1, :])
        hc_ref[0:1, :] = hlast
        ext_ref[0:HALO, :] = xr_ref[tc - HALO:tc, :]
        y = h_ref[...] * _gelu(gate_ref[...])
        yn_ref[...] = (y * _rms_scale(y, D_RNN) * gro_ref[...]).astype(BF16)

    row = lambda c: (c, 0)
    const = lambda c: (0, 0)
    vec = pl.BlockSpec((1, D_RNN), const)
    mat = pl.BlockSpec((D_RNN, D_RNN), const)
    return pl.pallas_call(
        body,
        name="rglru_fwd",
        grid=(s // tc,),
        in_specs=[
            pl.BlockSpec((tc, D_RNN), row), pl.BlockSpec((tc, D_RNN), row),
            pl.BlockSpec((CONV_W, D_RNN), const), vec, mat, vec, mat, vec, vec, vec,
        ],
        out_specs=[pl.BlockSpec((tc, D_RNN), row), pl.BlockSpec((tc, D_RNN), row)],
        out_shape=[jax.ShapeDtypeStruct((s, D_RNN), F32), jax.ShapeDtypeStruct((s, D_RNN), BF16)],
        scratch_shapes=[
            pltpu.VMEM((tc + HALO, D_RNN), F32),
            pltpu.VMEM((tc, D_RNN), F32),
            pltpu.VMEM((tc, D_RNN), F32),
            pltpu.VMEM((SUBLANES, D_RNN), F32),
        ],
        compiler_params=_cp("arbitrary"),
    )(xr, gate, cw, cb, wa, ba, wx, bx, lam, gro)


def _rg_bwd(xr, gate, h, dyn, cw, cb, wa, ba, wx, bx, lam, gro):
    s = xr.shape[0]
    tc = min(RG_CHUNK, s)
    nt = s // tc
    nsub = tc // SUBLANES
    per = tc // HALO

    def body(xr_ref, xrh_ref, gate_ref, h_ref, hh_ref, dyn_ref,
             cw_ref, cb_ref, wa_ref, ba_ref, wx_ref, bx_ref, lam_ref, gro_ref,
             dxr_ref, dgate_ref, dcw_ref, dcb_ref, dwa_ref, dba_ref, dwx_ref, dbx_ref, dlam_ref, dgro_ref,
             extx_ref, exth_ref, exta_ref, extd_ref, cs_ref, dh_ref, g_ref, gc_ref, an_ref, dn_ref):
        c = pl.program_id(0)
        first = c == 0
        is_start = c == nt - 1

        @pl.when(first)
        def _():
            gc_ref[...] = jnp.zeros_like(gc_ref)
            an_ref[...] = jnp.zeros_like(an_ref)
            dn_ref[...] = jnp.zeros_like(dn_ref)

        zero_halo = jnp.zeros((HALO, D_RNN), F32)
        extx_ref[0:HALO, :] = jnp.where(is_start, zero_halo, xrh_ref[...])
        extx_ref[HALO:HALO + tc, :] = xr_ref[...]
        exth_ref[0:HALO, :] = jnp.where(is_start, zero_halo, hh_ref[...])
        exth_ref[HALO:HALO + tc, :] = h_ref[...]

        xc = _conv(extx_ref, cw_ref, cb_ref, tc)
        xcb, r, ig, ls, a, em = _rg_gates(xc, wa_ref, ba_ref, wx_ref, bx_ref, lam_ref)
        mult = jnp.sqrt(em)

        hv = h_ref[...]
        gt = gate_ref[...]
        ge = _gelu(gt)
        y = hv * ge
        rr = _rms_scale(y, D_RNN)
        dynv = dyn_ref[...]
        dy = _rms_bwd(dynv, y, gro_ref[...], rr, D_RNN)
        _acc(dgro_ref, jnp.sum(dynv * y * rr, axis=0, keepdims=True), first)
        dgate_ref[...] = dy * hv * _gelu_grad(gt)
        dh_ref[...] = dy * ge

        exta_ref[0:tc, :] = a
        exta_ref[tc:tc + HALO, :] = an_ref[...]
        an_ref[...] = a[0:HALO, :]
        cs_ref[...] = exta_ref[pl.ds(1, tc), :]
        row = lax.broadcasted_iota(jnp.int32, (SUBLANES, D_RNN), 0)

        def step(k, gnext):
            kk = nsub - 1 - k
            rows = pl.ds(pl.multiple_of(kk * SUBLANES, SUBLANES), SUBLANES)
            cv = cs_ref[rows, :]
            yv = dh_ref[rows, :]
            for d in (1, 2, 4):
                up = SUBLANES - d
                c_sh = jnp.where(row < up, pltpu.roll(cv, up, 0), 1.0)
                y_sh = jnp.where(row < up, pltpu.roll(yv, up, 0), 0.0)
                yv = yv + cv * y_sh
                cv = cv * c_sh
            gv = yv + cv * gnext
            g_ref[rows, :] = gv
            return gv[0:1, :]

        gfirst = lax.fori_loop(0, nsub, step, gc_ref[0:1, :])
        gc_ref[0:1, :] = gfirst

        gsc = g_ref[...]
        hprev = exth_ref[pl.ds(HALO - 1, tc), :]
        da = gsc * hprev
        dib = gsc * mult
        dmult = gsc * (ig * xc)
        dlog_a = da * a - dmult * ((1.0 - em) / mult)
        dr = dlog_a * (RG_C * ls)
        dls = jnp.sum(dlog_a * (RG_C * r), axis=0, keepdims=True)
        _acc(dlam_ref, dls * _sigmoid(-lam_ref[...]), first)
        dpa = dr * r * (1.0 - r)
        dpx = (dib * xc) * ig * (1.0 - ig)
        dpab = dpa.astype(BF16)
        dpxb = dpx.astype(BF16)
        dxc = dib * ig + _nt(dpab, wa_ref[...]) + _nt(dpxb, wx_ref[...])
        _acc(dwa_ref, _tn(xcb, dpab), first)
        _acc(dwx_ref, _tn(xcb, dpxb), first)
        _acc(dba_ref, jnp.sum(dpa, axis=0, keepdims=True), first)
        _acc(dbx_ref, jnp.sum(dpx, axis=0, keepdims=True), first)
        _acc(dcb_ref, jnp.sum(dxc, axis=0, keepdims=True), first)

        extd_ref[0:tc, :] = dxc
        extd_ref[tc:tc + HALO, :] = dn_ref[...]
        dn_ref[...] = dxc[0:HALO, :]
        dxr = cw_ref[0:1, :] * extd_ref[pl.ds(3, tc), :]
        for j in range(1, CONV_W):
            dxr = dxr + cw_ref[j:j + 1, :] * extd_ref[pl.ds(3 - j, tc), :]
        dxr_ref[...] = dxr
        dcw = jnp.concatenate(
            [jnp.sum(dxc * extx_ref[pl.ds(HALO - 3 + j, tc), :], axis=0, keepdims=True) for j in range(CONV_W)],
            axis=0)
        _acc(dcw_ref, dcw, first)

    rev = lambda c: (nt - 1 - c, 0)
    halo = lambda c: (jnp.maximum((nt - 1 - c) * per - 1, 0), 0)
    const = lambda c: (0, 0)
    vec = pl.BlockSpec((1, D_RNN), const)
    mat = pl.BlockSpec((D_RNN, D_RNN), const)
    chunk = pl.BlockSpec((tc, D_RNN), rev)
    vs = jax.ShapeDtypeStruct((1, D_RNN), F32)
    ms = jax.ShapeDtypeStruct((D_RNN, D_RNN), F32)
    return pl.pallas_call(
        body,
        name="rglru_bwd",
        grid=(nt,),
        in_specs=[
            chunk, pl.BlockSpec((HALO, D_RNN), halo), chunk, chunk, pl.BlockSpec((HALO, D_RNN), halo), chunk,
            pl.BlockSpec((CONV_W, D_RNN), const), vec, mat, vec, mat, vec, vec, vec,
        ],
        out_specs=[chunk, chunk, pl.BlockSpec((CONV_W, D_RNN), const), vec, mat, vec, mat, vec, vec, vec],
        out_shape=[
            jax.ShapeDtypeStruct((s, D_RNN), F32), jax.ShapeDtypeStruct((s, D_RNN), F32),
            jax.ShapeDtypeStruct((CONV_W, D_RNN), F32), vs, ms, vs, ms, vs, vs, vs,
        ],
        scratch_shapes=[
            pltpu.VMEM((tc + HALO, D_RNN), F32),
            pltpu.VMEM((tc + HALO, D_RNN), F32),
            pltpu.VMEM((tc + HALO, D_RNN), F32),
            pltpu.VMEM((tc + HALO, D_RNN), F32),
            pltpu.VMEM((tc, D_RNN), F32),
            pltpu.VMEM((tc, D_RNN), F32),
            pltpu.VMEM((tc, D_RNN), F32),
            pltpu.VMEM((SUBLANES, D_RNN), F32),
            pltpu.VMEM((HALO, D_RNN), F32),
            pltpu.VMEM((HALO, D_RNN), F32),
        ],
        compiler_params=_cp("arbitrary"),
    )(xr, xr, gate, h, h, dyn, cw, cb, wa, ba, wx, bx, lam, gro)


ATT_BLOCK = 256
ATT_SCALE = 1.0 / math.sqrt(HEAD_DIM)


def _split_dot(v, m):
    hi = v.astype(BF16)
    lo = (v - hi.astype(F32)).astype(BF16)
    return _nn(hi, m) + _nn(lo, m)


def _log_terms(z):
    t = jnp.exp(-jnp.abs(z))
    lg = jnp.log(1.0 + t)
    return jnp.minimum(z, 0.0) - lg, -jnp.maximum(z, 0.0) - lg, t


def _attn_fwd(q, k, v, gq, gk):
    s = q.shape[1]
    blk = min(ATT_BLOCK, s)

    def body(q_ref, k_ref, v_ref, gq_ref, gk_ref, ol_ref, kn_ref, vb_ref):
        qi = pl.program_id(1)

        @pl.when(qi == 0)
        def _():
            kv = k_ref[0]
            kn_ref[...] = (kv * _rms_scale(kv, HEAD_DIM) * gk_ref[...]).astype(BF16)
            vb_ref[...] = v_ref[0].astype(BF16)

        qv = q_ref[0]
        qn = ((qv * _rms_scale(qv, HEAD_DIM) * gq_ref[...]) * ATT_SCALE).astype(BF16)
        ri = lax.broadcasted_iota(jnp.int32, (blk, blk), 0)
        ci = lax.broadcasted_iota(jnp.int32, (blk, blk), 1)
        after = (ri > ci).astype(BF16)
        causal = ci < ri

        def block(kj, carry, acc, diag):
            rows = pl.ds(pl.multiple_of(kj * blk, blk), blk)
            z = _nt(qn, kn_ref[rows, :])
            lb, l1, _ = _log_terms(z)
            if diag:
                l1 = jnp.where(causal, l1, 0.0)
            cs = _split_dot(l1, after)
            w = jnp.exp(lb + cs + carry)
            if diag:
                w = jnp.where(causal, w, 0.0)
            acc = acc + _nn(w.astype(BF16), vb_ref[rows, :])
            carry = carry + cs[:, 0:1] + l1[:, 0:1]
            return carry, acc

        carry, acc = block(qi, jnp.zeros((blk, 1), F32), jnp.zeros((blk, HEAD_DIM), F32), True)
        carry, acc = lax.fori_loop(
            0, qi, lambda t, ca: block(qi - 1 - t, ca[0], ca[1], False), (carry, acc))
        ol_ref[0] = jnp.concatenate([acc, jnp.broadcast_to(carry, (blk, HEAD_DIM))], axis=1)

    head = pl.BlockSpec((1, s, HEAD_DIM), lambda hh, i: (hh, 0, 0))
    gain = pl.BlockSpec((1, HEAD_DIM), lambda hh, i: (0, 0))
    return pl.pallas_call(
        body,
        name="attn_fwd",
        grid=(N_HEADS, s // blk),
        in_specs=[pl.BlockSpec((1, blk, HEAD_DIM), lambda hh, i: (hh, i, 0)), head, head, gain, gain],
        out_specs=pl.BlockSpec((1, blk, LANES), lambda hh, i: (hh, i, 0)),
        out_shape=jax.ShapeDtypeStruct((N_HEADS, s, LANES), F32),
        scratch_shapes=[pltpu.VMEM((s, HEAD_DIM), BF16), pltpu.VMEM((s, HEAD_DIM), BF16)],
        compiler_params=_cp("arbitrary", "arbitrary"),
    )(q, k, v, gq, gk)


def _attn_bwd(q, k, v, gq, gk, ol, do):
    s = q.shape[1]
    blk = min(ATT_BLOCK, s)
    ni = s // blk

    def body(q_ref, k_ref, v_ref, gq_ref, gk_ref, ol_ref, do_ref,
             dq_ref, dk_ref, dv_ref, dgq_ref, dgk_ref, kn_ref, vb_ref, dkn_ref):
        hh = pl.program_id(0)
        qi = pl.program_id(1)

        @pl.when(qi == 0)
        def _():
            kv = k_ref[0]
            kn_ref[...] = (kv * _rms_scale(kv, HEAD_DIM) * gk_ref[...]).astype(BF16)
            vb_ref[...] = v_ref[0].astype(BF16)
            dkn_ref[...] = jnp.zeros_like(dkn_ref)
            dv_ref[...] = jnp.zeros_like(dv_ref)

        qv = q_ref[0]
        rq = _rms_scale(qv, HEAD_DIM)
        qn = ((qv * rq * gq_ref[...]) * ATT_SCALE).astype(BF16)
        dov = do_ref[0]
        dob = dov.astype(BF16)
        ltot = ol_ref[0][:, HEAD_DIM:HEAD_DIM + 1]
        ri = lax.broadcasted_iota(jnp.int32, (blk, blk), 0)
        ci = lax.broadcasted_iota(jnp.int32, (blk, blk), 1)
        before = (ri < ci).astype(BF16)
        causal = ci < ri

        def block(kj, cl, ce, dqn, diag):
            rows = pl.ds(pl.multiple_of(kj * blk, blk), blk)
            kb = kn_ref[rows, :]
            vbk = vb_ref[rows, :]
            z = _nt(qn, kb)
            lb, l1, t = _log_terms(z)
            if diag:
                l1 = jnp.where(causal, l1, 0.0)
            pre = _split_dot(l1, before)
            tail = ltot - cl - pre - l1
            w = jnp.exp(lb + tail)
            if diag:
                w = jnp.where(causal, w, 0.0)
            wb = w.astype(BF16)
            dv_ref[0, rows, :] += _tn(wb, dob)
            e = w * _nt(dob, vbk)
            epre = _split_dot(e, before)
            inv = 1.0 / (1.0 + t)
            pos = z >= 0.0
            sig = jnp.where(pos, inv, t * inv)
            oms = jnp.where(pos, t * inv, inv)
            dz = e * oms - (ce + epre) * sig
            if diag:
                dz = jnp.where(causal, dz, 0.0)
            dzb = dz.astype(BF16)
            dqn = dqn + _nn(dzb, kb)
            dkn_ref[rows, :] += _tn(dzb, qn)
            cl = cl + pre[:, blk - 1:blk] + l1[:, blk - 1:blk]
            ce = ce + epre[:, blk - 1:blk] + e[:, blk - 1:blk]
            return cl, ce, dqn

        zero = jnp.zeros((blk, 1), F32)
        cl, ce, dqn = lax.fori_loop(
            0, qi, lambda t, c: block(t, c[0], c[1], c[2], False), (zero, zero, jnp.zeros((blk, HEAD_DIM), F32)))
        _, _, dqn = block(qi, cl, ce, dqn, True)

        first = jnp.logical_and(hh == 0, qi == 0)
        gq = gq_ref[...]
        dqs = dqn * ATT_SCALE
        dq_ref[0] = _rms_bwd(dqs, qv, gq, rq, HEAD_DIM)
        _acc(dgq_ref, jnp.sum(dqs * qv * rq, axis=0, keepdims=True), first)

        @pl.when(qi == ni - 1)
        def _():
            kv = k_ref[0]
            rk = _rms_scale(kv, HEAD_DIM)
            dkn = dkn_ref[...]
            dk_ref[0] = _rms_bwd(dkn, kv, gk_ref[...], rk, HEAD_DIM)
            _acc(dgk_ref, jnp.sum(dkn * kv * rk, axis=0, keepdims=True), hh == 0)

    head = pl.BlockSpec((1, s, HEAD_DIM), lambda hh, i: (hh, 0, 0))
    qblk = pl.BlockSpec((1, blk, HEAD_DIM), lambda hh, i: (hh, i, 0))
    gain = pl.BlockSpec((1, HEAD_DIM), lambda hh, i: (0, 0))
    hs = jax.ShapeDtypeStruct((N_HEADS, s, HEAD_DIM), F32)
    gs = jax.ShapeDtypeStruct((1, HEAD_DIM), F32)
    return pl.pallas_call(
        body,
        name="attn_bwd",
        grid=(N_HEADS, ni),
        in_specs=[qblk, head, head, gain, gain, pl.BlockSpec((1, blk, LANES), lambda hh, i: (hh, i, 0)), qblk],
        out_specs=[qblk, head, head, gain, gain],
        out_shape=[hs, hs, hs, gs, gs],
        scratch_shapes=[
            pltpu.VMEM((s, HEAD_DIM), BF16), pltpu.VMEM((s, HEAD_DIM), BF16), pltpu.VMEM((s, HEAD_DIM), F32)],
        compiler_params=_cp("arbitrary", "arbitrary"),
    )(q, k, v, gq, gk, ol, do)


def _att_norm(ol_ref, g_ref):
    ov = [ol_ref[hh][:, 0:HEAD_DIM] for hh in range(N_HEADS)]
    ss = ov[0] * ov[0]
    for hh in range(1, N_HEADS):
        ss = ss + ov[hh] * ov[hh]
    ra = lax.rsqrt(jnp.sum(ss, axis=-1, keepdims=True) * (1.0 / D_ATT) + EPS)
    ya = jnp.concatenate([ov[hh] * ra * g_ref[hh] for hh in range(N_HEADS)], axis=1).astype(BF16)
    return ov, ra, ya


def _out_fwd(x1, ynr, ol, gao, wout):
    s = x1.shape[0]
    tm = min(512, s)

    def body(x_ref, ynr_ref, ol_ref, g_ref, w_ref, x2_ref):
        _, _, ya = _att_norm(ol_ref, g_ref)
        x2_ref[...] = x_ref[...] + _nn(ynr_ref[...], w_ref[0:D_RNN, :]) + _nn(ya, w_ref[D_RNN:D_MODEL, :])

    row = lambda i: (i, 0)
    return pl.pallas_call(
        body,
        name="mix_out_fwd",
        grid=(s // tm,),
        in_specs=[
            pl.BlockSpec((tm, D_MODEL), row),
            pl.BlockSpec((tm, D_RNN), row),
            pl.BlockSpec((N_HEADS, tm, LANES), lambda i: (0, i, 0)),
            pl.BlockSpec((N_HEADS, 1, HEAD_DIM), lambda i: (0, 0, 0)),
            pl.BlockSpec((D_MODEL, D_MODEL), lambda i: (0, 0)),
        ],
        out_specs=pl.BlockSpec((tm, D_MODEL), row),
        out_shape=jax.ShapeDtypeStruct((s, D_MODEL), F32),
        compiler_params=_cp("arbitrary"),
    )(x1, ynr, ol, gao, wout)


def _out_bwd(dx2b, ynr, ol, gao, wout):
    s = ynr.shape[0]
    tm = min(512, s)

    def body(dx_ref, ynr_ref, ol_ref, g_ref, w_ref, dynr_ref, do_ref, dw_ref, dg_ref):
        first = pl.program_id(0) == 0
        dxb = dx_ref[...]
        dynr_ref[...] = _nt(dxb, w_ref[0:D_RNN, :])
        dya = _nt(dxb, w_ref[D_RNN:D_MODEL, :])
        ov, ra, ya = _att_norm(ol_ref, g_ref)
        c_r = _tn(ynr_ref[...], dxb)
        c_a = _tn(ya, dxb)

        @pl.when(first)
        def _():
            dw_ref[0:D_RNN, :] = c_r
            dw_ref[D_RNN:D_MODEL, :] = c_a

        @pl.when(jnp.logical_not(first))
        def _():
            dw_ref[0:D_RNN, :] += c_r
            dw_ref[D_RNN:D_MODEL, :] += c_a

        dyh = [dya[:, HEAD_DIM * hh:HEAD_DIM * (hh + 1)] for hh in range(N_HEADS)]
        uo = dyh[0] * g_ref[0] * ov[0]
        for hh in range(1, N_HEADS):
            uo = uo + dyh[hh] * g_ref[hh] * ov[hh]
        m = jnp.sum(uo, axis=-1, keepdims=True) * (1.0 / D_ATT)
        r3m = ra * ra * ra * m
        for hh in range(N_HEADS):
            do_ref[hh] = ra * (dyh[hh] * g_ref[hh]) - ov[hh] * r3m
            contrib = jnp.sum(dyh[hh] * ov[hh] * ra, axis=0, keepdims=True)

            @pl.when(first)
            def _():
                dg_ref[hh] = contrib

            @pl.when(jnp.logical_not(first))
            def _():
                dg_ref[hh] += contrib

    row = lambda i: (i, 0)
    return pl.pallas_call(
        body,
        name="mix_out_bwd",
        grid=(s // tm,),
        in_specs=[
            pl.BlockSpec((tm, D_MODEL), row),
            pl.BlockSpec((tm, D_RNN), row),
            pl.BlockSpec((N_HEADS, tm, LANES), lambda i: (0, i, 0)),
            pl.BlockSpec((N_HEADS, 1, HEAD_DIM), lambda i: (0, 0, 0)),
            pl.BlockSpec((D_MODEL, D_MODEL), lambda i: (0, 0)),
        ],
        out_specs=[
            pl.BlockSpec((tm, D_RNN), row),
            _heads_spec(tm),
            pl.BlockSpec((D_MODEL, D_MODEL), lambda i: (0, 0)),
            pl.BlockSpec((N_HEADS, 1, HEAD_DIM), lambda i: (0, 0, 0)),
        ],
        out_shape=[
            jax.ShapeDtypeStruct((s, D_RNN), F32),
            jax.ShapeDtypeStruct((N_HEADS, s, HEAD_DIM), F32),
            jax.ShapeDtypeStruct((D_MODEL, D_MODEL), F32),
            jax.ShapeDtypeStruct((N_HEADS, 1, HEAD_DIM), F32),
        ],
        compiler_params=_cp("arbitrary"),
    )(dx2b, ynr, ol, gao, wout)


def _loss_head(y, target):
    s = y.shape[0]
    tm = min(512, s)

    def body(y_ref, t_ref, dy_ref, dyb_ref, loss_ref):
        d = y_ref[...] - t_ref[...]
        dy = d * (1.0 / D_MODEL)
        dy_ref[...] = dy
        dyb_ref[...] = dy.astype(BF16)
        part = 0.5 * jnp.sum(jnp.sum(d * d, axis=-1, keepdims=True) * (1.0 / D_MODEL), axis=0, keepdims=True)
        _acc(loss_ref, jnp.broadcast_to(part, (SUBLANES, LANES)), pl.program_id(0) == 0)

    row = lambda i: (i, 0)
    return pl.pallas_call(
        body,
        name="loss_head",
        grid=(s // tm,),
        in_specs=[pl.BlockSpec((tm, D_MODEL), row), pl.BlockSpec((tm, D_MODEL), row)],
        out_specs=[
            pl.BlockSpec((tm, D_MODEL), row), pl.BlockSpec((tm, D_MODEL), row),
            pl.BlockSpec((SUBLANES, LANES), lambda i: (0, 0)),
        ],
        out_shape=[
            jax.ShapeDtypeStruct((s, D_MODEL), F32), jax.ShapeDtypeStruct((s, D_MODEL), BF16),
            jax.ShapeDtypeStruct((SUBLANES, LANES), F32),
        ],
        compiler_params=_cp("arbitrary"),
    )(y, target)


def _block_diag(w):
    out = jnp.zeros((D_RNN, D_RNN), w.dtype)
    for nblk in range(w.shape[0]):
        lo = nblk * 64
        out = lax.dynamic_update_slice(out, w[nblk], (lo, lo))
    return out


def _diag_blocks(m):
    return jnp.stack([m[64 * nblk:64 * (nblk + 1), 64 * nblk:64 * (nblk + 1)] for nblk in range(8)])


def _local_step(x, target, p):
    wa = _block_diag(p["rg_w_a"]).astype(BF16)
    wx = _block_diag(p["rg_w_x"]).astype(BF16)
    gao = p["attn_out_norm"].reshape(N_HEADS, 1, HEAD_DIM)
    rg_args = (p["conv_w"], p["conv_b"], wa, p["rg_b_a"], wx, p["rg_b_x"], p["rg_lambda"], p["rnn_out_norm"])

    x1, n1, gg1, uu1 = _ffn_fwd(x, p["ffn1_norm"], p["wg1t"], p["wu1t"], p["wd1"], "ffn1_fwd")
    hmix, xr, gate, q, k, v = _in_fwd(x1, p["mix_norm"], p["wint"])
    hrec, ynr = _rg_fwd(xr, gate, *rg_args)
    ol = _attn_fwd(q, k, v, p["q_norm"], p["k_norm"])
    x2 = _out_fwd(x1, ynr, ol, gao, p["wout"])
    x3, n2, gg2, uu2 = _ffn_fwd(x2, p["ffn2_norm"], p["wg2t"], p["wu2t"], p["wd2"], "ffn2_fwd")
    dy3, dy3b, loss_tile = _loss_head(x3, target)

    g = {}
    dx2, dx2b, g["wg2t"], g["wu2t"], g["wd2"], g["ffn2_norm"] = _ffn_bwd(
        dy3b, dy3, x2, p["ffn2_norm"], n2, gg2, uu2, p["wg2t"], p["wu2t"], p["wd2"], "ffn2_bwd")
    dynr, do, g["wout"], dgao = _out_bwd(dx2b, ynr, ol, gao, p["wout"])
    g["attn_out_norm"] = dgao.reshape(1, D_ATT)
    dq, dk, dv, g["q_norm"], g["k_norm"] = _attn_bwd(q, k, v, p["q_norm"], p["k_norm"], ol, do)
    (dxr, dgate, g["conv_w"], g["conv_b"], dwa, g["rg_b_a"], dwx, g["rg_b_x"], g["rg_lambda"],
     g["rnn_out_norm"]) = _rg_bwd(xr, gate, hrec, dynr, *rg_args)
    g["rg_w_a"] = _diag_blocks(dwa)
    g["rg_w_x"] = _diag_blocks(dwx)
    dx1, dx1b, g["wint"], g["mix_norm"] = _in_bwd(dx2, x1, p["mix_norm"], hmix, p["wint"], dxr, dgate, dq, dk, dv)
    dx0, _, g["wg1t"], g["wu1t"], g["wd1"], g["ffn1_norm"] = _ffn_bwd(
        dx1b, dx1, x, p["ffn1_norm"], n1, gg1, uu1, p["wg1t"], p["wu1t"], p["wd1"], "ffn1_bwd")
    return loss_tile[0, 0], dx0, g


ANY = pl.BlockSpec(memory_space=pl.ANY)


def _me():
    return 4 * lax.axis_index("x") + 2 * lax.axis_index("y") + lax.axis_index("c")


def _peer(k):
    x, y, c = lax.axis_index("x"), lax.axis_index("y"), lax.axis_index("c")
    px = 1 - x if k & 4 else x
    py = 1 - y if k & 2 else y
    pc = 1 - c if k & 1 else c
    return (px, py, pc), 4 * px + 2 * py + pc


def _all_gather_rows(blocks, name):
    n = len(blocks)

    def body(*refs):
        ins, outs = refs[:n], refs[n:2 * n]
        send, recv, loc = refs[2 * n:]
        me = _me()

        def rows(a, dev):
            r = ins[a].shape[0]
            return outs[a].at[pl.ds(pl.multiple_of(dev * r, SUBLANES), r), :]

        def remote(a, k, dev):
            peer, _ = _peer(k)
            return pltpu.make_async_remote_copy(
                src_ref=ins[a], dst_ref=rows(a, dev), send_sem=send.at[a, k - 1], recv_sem=recv.at[a, k - 1],
                device_id=peer, device_id_type=MESH)

        local = [pltpu.make_async_copy(ins[a], rows(a, me), loc.at[a]) for a in range(n)]
        for a in range(n):
            local[a].start()
            for k in range(1, N_DEV):
                remote(a, k, me).start()
        for a in range(n):
            for k in range(1, N_DEV):
                remote(a, k, _peer(k)[1]).wait()
            local[a].wait()

    return pl.pallas_call(
        body,
        name=name,
        in_specs=[ANY] * n,
        out_specs=[ANY] * n,
        out_shape=[jax.ShapeDtypeStruct((N_DEV * b.shape[0], b.shape[1]), b.dtype) for b in blocks],
        scratch_shapes=[
            pltpu.SemaphoreType.DMA((n, N_DEV - 1)),
            pltpu.SemaphoreType.DMA((n, N_DEV - 1)),
            pltpu.SemaphoreType.DMA((n,)),
        ],
    )(*blocks)


def _exchange_partials(parts, name):
    n = len(parts)

    def body(*refs):
        ins, outs = refs[:n], refs[n:2 * n]
        send, recv, loc = refs[2 * n:]
        me = _me()

        def rows(a, dev):
            r = ins[a].shape[0] // N_DEV
            return ins[a].at[pl.ds(pl.multiple_of(dev * r, SUBLANES), r), :]

        def remote(a, k, slot):
            peer, pid = _peer(k)
            return pltpu.make_async_remote_copy(
                src_ref=rows(a, pid), dst_ref=outs[a].at[slot], send_sem=send.at[a, k - 1], recv_sem=recv.at[a, k - 1],
                device_id=peer, device_id_type=MESH)

        local = [pltpu.make_async_copy(rows(a, me), outs[a].at[me], loc.at[a]) for a in range(n)]
        for a in range(n):
            local[a].start()
            for k in range(1, N_DEV):
                remote(a, k, me).start()
        for a in range(n):
            for k in range(1, N_DEV):
                remote(a, k, _peer(k)[1]).wait()
            local[a].wait()

    return pl.pallas_call(
        body,
        name=name,
        in_specs=[ANY] * n,
        out_specs=[ANY] * n,
        out_shape=[jax.ShapeDtypeStruct((N_DEV, b.shape[0] // N_DEV, b.shape[1]), b.dtype) for b in parts],
        scratch_shapes=[
            pltpu.SemaphoreType.DMA((n, N_DEV - 1)),
            pltpu.SemaphoreType.DMA((n, N_DEV - 1)),
            pltpu.SemaphoreType.DMA((n,)),
        ],
    )(*parts)


def _row_tile(rows, cap):
    best = SUBLANES
    for t in range(SUBLANES, min(rows, cap) + 1, SUBLANES):
        if rows % t == 0:
            best = t
    return best


def _sum_slots(land, name):
    _, r, c = land.shape
    tm = _row_tile(r, 256)

    def body(l_ref, o_ref):
        acc = l_ref[0]
        for d in range(1, N_DEV):
            acc = acc + l_ref[d]
        o_ref[...] = acc

    return pl.pallas_call(
        body,
        name=name,
        grid=(r // tm,),
        in_specs=[pl.BlockSpec((N_DEV, tm, c), lambda i: (0, i, 0))],
        out_specs=pl.BlockSpec((tm, c), lambda i: (i, 0)),
        out_shape=jax.ShapeDtypeStruct((r, c), F32),
        compiler_params=_cp("arbitrary"),
    )(land)


def _adamw(w, g, m, v, name):
    r, c = w.shape
    tm = _row_tile(r, 512)

    def body(w_ref, g_ref, m_ref, v_ref, d_ref, nm_ref, nv_ref):
        gv = g_ref[...]
        nm = ADAM_B1 * m_ref[...] + (1.0 - ADAM_B1) * gv
        nv = ADAM_B2 * v_ref[...] + (1.0 - ADAM_B2) * (gv * gv)
        m_hat = nm / (1.0 - ADAM_B1 ** ADAM_STEP)
        v_hat = nv / (1.0 - ADAM_B2 ** ADAM_STEP)
        d_ref[...] = -ADAM_LR * (m_hat / (jnp.sqrt(v_hat) + ADAM_EPS) + ADAM_WD * w_ref[...])
        nm_ref[...] = nm
        nv_ref[...] = nv

    spec = pl.BlockSpec((tm, c), lambda i: (i, 0))
    shape = jax.ShapeDtypeStruct((r, c), F32)
    return pl.pallas_call(
        body,
        name=name,
        grid=(r // tm,),
        in_specs=[spec] * 4,
        out_specs=[spec] * 3,
        out_shape=[shape] * 3,
        compiler_params=_cp("arbitrary"),
    )(w, g, m, v)


SMALL = ("ffn1_norm", "mix_norm", "conv_w", "conv_b", "rg_w_a", "rg_b_a", "rg_w_x", "rg_b_x", "rg_lambda",
         "q_norm", "k_norm", "rnn_out_norm", "attn_out_norm", "ffn2_norm")
PACK_GRANULE = SUBLANES * LANES


def _pack(arrs):
    flat = jnp.concatenate([a.reshape(-1) for a in arrs])
    pad = -flat.shape[0] % PACK_GRANULE
    return jnp.pad(flat, (0, pad)).reshape(-1, LANES)


def _unpack(buf, shapes):
    flat = buf.reshape(-1)
    out, off = [], 0
    for shp in shapes:
        size = math.prod(shp)
        out.append(flat[off:off + size].reshape(shp))
        off += size
    return out


def kernel(x, ffn1_norm, ffn1_w_gate, ffn1_w_up, ffn1_w_down, mix_norm, w_in, conv_w, conv_b, rg_w_a, rg_b_a, rg_w_x, rg_b_x, rg_lambda, q_norm, k_norm, rnn_out_norm, attn_out_norm, w_out, ffn2_norm, ffn2_w_gate, ffn2_w_up, ffn2_w_down, loss_target, m_ffn1_norm, m_ffn1_w_gate, m_ffn1_w_up, m_ffn1_w_down, m_mix_norm, m_w_in, m_conv_w, m_conv_b, m_rg_w_a, m_rg_b_a, m_rg_w_x, m_rg_b_x, m_rg_lambda, m_q_norm, m_k_norm, m_rnn_out_norm, m_attn_out_norm, m_w_out, m_ffn2_norm, m_ffn2_w_gate, m_ffn2_w_up, m_ffn2_w_down, v_ffn1_norm, v_ffn1_w_gate, v_ffn1_w_up, v_ffn1_w_down, v_mix_norm, v_w_in, v_conv_w, v_conv_b, v_rg_w_a, v_rg_b_a, v_rg_w_x, v_rg_b_x, v_rg_lambda, v_q_norm, v_k_norm, v_rnn_out_norm, v_attn_out_norm, v_w_out, v_ffn2_norm, v_ffn2_w_gate, v_ffn2_w_up, v_ffn2_w_down):
    given = dict(locals())
    names = ("ffn1_norm", "ffn1_w_gate", "ffn1_w_up", "ffn1_w_down", "mix_norm", "w_in", "conv_w", "conv_b",
             "rg_w_a", "rg_b_a", "rg_w_x", "rg_b_x", "rg_lambda", "q_norm", "k_norm", "rnn_out_norm",
             "attn_out_norm", "w_out", "ffn2_norm", "ffn2_w_gate", "ffn2_w_up", "ffn2_w_down")
    me = _me()

    transposed = {"ffn1_w_gate": "wg1t", "ffn1_w_up": "wu1t", "w_in": "wint", "ffn2_w_gate": "wg2t", "ffn2_w_up": "wu2t"}
    straight = {"ffn1_w_down": "wd1", "w_out": "wout", "ffn2_w_down": "wd2"}
    big = {**transposed, **straight}
    keys = list(big.values())
    shards = [given[n][0].T.astype(BF16) if n in transposed else given[n][0].astype(BF16) for n in big]
    cw_tile = jnp.zeros((SUBLANES, LANES), F32).at[0:CONV_W, 0:D_RNN // N_DEV].set(conv_w[0])
    gathered = _all_gather_rows(shards + [cw_tile], "gather_weights")
    p = dict(zip(keys, gathered[:-1]))
    cw_all = gathered[-1].reshape(N_DEV, SUBLANES, LANES)[:, 0:CONV_W, 0:D_RNN // N_DEV]
    p["conv_w"] = jnp.transpose(cw_all, (1, 0, 2)).reshape(CONV_W, D_RNN)
    p["rg_w_a"] = rg_w_a[0]
    p["rg_w_x"] = rg_w_x[0]
    for n in ("ffn1_norm", "mix_norm", "conv_b", "rg_b_a", "rg_b_x", "rg_lambda", "q_norm", "k_norm",
              "rnn_out_norm", "attn_out_norm", "ffn2_norm"):
        p[n] = given[n]

    loss_local, dx, g = _local_step(x[0], loss_target[0], p)
    loss = lax.psum(loss_local, ("x", "y", "c"))

    landed = _exchange_partials([g[k] for k in keys], "exchange_grads")
    grads, deltas, new_m, new_v = {}, {}, {}, {}
    for n, key, land in zip(big, keys, landed):
        gsum = _sum_slots(land, "sum_" + key)
        if n in transposed:
            gsum = gsum.T
        grads[n] = gsum[None]
        d, nm, nv = _adamw(given[n][0], gsum, given["m_" + n][0], given["v_" + n][0], "adamw_" + key)
        deltas[n], new_m[n], new_v[n] = d[None], nm[None], nv[None]

    small_shapes = [(1, CONV_W, D_RNN) if n == "conv_w" else given[n].shape for n in SMALL]
    packed = _pack([g[n] for n in SMALL])
    rows = packed.shape[0]
    all_small = _all_gather_rows([packed], "gather_small_grads")[0].reshape(N_DEV, rows, LANES)
    summed = dict(zip(SMALL, _unpack(_sum_slots(all_small, "sum_small"), small_shapes)))
    summed["conv_w"] = lax.dynamic_slice_in_dim(summed["conv_w"], me * (D_RNN // N_DEV), D_RNN // N_DEV, axis=2)
    local_shapes = [given[n].shape for n in SMALL]
    d, nm, nv = _adamw(_pack([given[n] for n in SMALL]), _pack([summed[n] for n in SMALL]),
                       _pack([given["m_" + n] for n in SMALL]), _pack([given["v_" + n] for n in SMALL]), "adamw_small")
    for n, dd, mm, vv in zip(SMALL, _unpack(d, local_shapes), _unpack(nm, local_shapes), _unpack(nv, local_shapes)):
        grads[n], deltas[n], new_m[n], new_v[n] = summed[n], dd, mm, vv

    return (loss, dx[None], *[grads[n] for n in names], *[deltas[n] for n in names],
            *[new_m[n] for n in names], *[new_v[n] for n in names])
```

```python
import functools
import math

import jax
import jax.numpy as jnp
from jax import lax
from jax.experimental import pallas as pl
from jax.experimental.pallas import tpu as pltpu

F32 = jnp.float32
BF16 = jnp.bfloat16

D_MODEL = 1024
D_FF = 2816
D_RNN = 512
D_ATT = 512
N_HEADS = 8
HEAD_DIM = 64
N_IN = 2 * D_RNN + 3 * D_ATT
CONV_W = 4
RG_C = 8.0
EPS = 1e-6
N_DEV = 8

ADAM_LR = 0.001
ADAM_B1 = 0.9
ADAM_B2 = 0.999
ADAM_EPS = 1e-08
ADAM_WD = 0.01
ADAM_STEP = 10

VMEM_LIMIT_BYTES = 56 * 1024 * 1024
LANES = 128
SUBLANES = 8

MESH = pl.DeviceIdType.MESH


def _cp(*sem):
    return pltpu.CompilerParams(dimension_semantics=sem, vmem_limit_bytes=VMEM_LIMIT_BYTES)


def _nn(a, b):
    return jnp.dot(a, b, preferred_element_type=F32)


def _nt(a, b):
    return lax.dot_general(a, b, (((1,), (1,)), ((), ())), preferred_element_type=F32)


def _tn(a, b):
    return lax.dot_general(a, b, (((0,), (0,)), ((), ())), preferred_element_type=F32)


def _rms_scale(x, width):
    return lax.rsqrt(jnp.sum(x * x, axis=-1, keepdims=True) * (1.0 / width) + EPS)


def _rms_bwd(dn, x, g, r, width):
    u = dn * g
    m = jnp.sum(u * x, axis=-1, keepdims=True) * (1.0 / width)
    return r * u - x * (r * r * r) * m


def _acc(ref, val, first):
    @pl.when(first)
    def _():
        ref[...] = val

    @pl.when(jnp.logical_not(first))
    def _():
        ref[...] += val


def _sigmoid(x):
    return jax.nn.sigmoid(x)


_GELU_C = math.sqrt(2.0 / math.pi)


def _gelu(x):
    return 0.5 * x * (1.0 + jnp.tanh(_GELU_C * (x + 0.044715 * (x * x * x))))


def _gelu_grad(x):
    t = jnp.tanh(_GELU_C * (x + 0.044715 * (x * x * x)))
    return 0.5 * (1.0 + t) + 0.5 * x * (1.0 - t * t) * (_GELU_C * (1.0 + 3.0 * 0.044715 * (x * x)))


def _log_sigmoid(x):
    return jnp.minimum(x, 0.0) - jnp.log(1.0 + jnp.exp(-jnp.abs(x)))


def _neg_expm1(x):
    series = x * (1.0 + x * (0.5 + x * (1.0 / 6.0 + x * (1.0 / 24.0))))
    return -jnp.where(x > -0.01, series, jnp.exp(x) - 1.0)


FFN_TF = 256


def _ffn_fwd(x, g, wgt, wut, wd, name):
    s = x.shape[0]
    tm = min(1024, s)
    nj = D_FF // FFN_TF

    def body(x_ref, g_ref, wg_ref, wu_ref, wd_ref, xo_ref, n_ref, gg_ref, uu_ref, acc_ref):
        j = pl.program_id(1)

        @pl.when(j == 0)
        def _():
            xv = x_ref[...]
            n_ref[...] = (xv * _rms_scale(xv, D_MODEL) * g_ref[...]).astype(BF16)
            acc_ref[...] = jnp.zeros_like(acc_ref)

        n = n_ref[...]
        gg = _nt(n, wg_ref[...])
        uu = _nt(n, wu_ref[...])
        gg_ref[...] = gg.astype(BF16)
        uu_ref[...] = uu.astype(BF16)
        act = (gg * _sigmoid(gg) * uu).astype(BF16)
        acc_ref[...] += _nn(act, wd_ref[...])

        @pl.when(j == nj - 1)
        def _():
            xo_ref[...] = x_ref[...] + 0.5 * acc_ref[...]

    row = lambda i, j: (i, 0)
    wrow = lambda i, j: (j, 0)
    return pl.pallas_call(
        body,
        name=name,
        grid=(s // tm, nj),
        in_specs=[
            pl.BlockSpec((tm, D_MODEL), row),
            pl.BlockSpec((1, D_MODEL), lambda i, j: (0, 0)),
            pl.BlockSpec((FFN_TF, D_MODEL), wrow),
            pl.BlockSpec((FFN_TF, D_MODEL), wrow),
            pl.BlockSpec((FFN_TF, D_MODEL), wrow),
        ],
        out_specs=[
            pl.BlockSpec((tm, D_MODEL), row),
            pl.BlockSpec((tm, D_MODEL), row),
            pl.BlockSpec((tm, FFN_TF), lambda i, j: (i, j)),
            pl.BlockSpec((tm, FFN_TF), lambda i, j: (i, j)),
        ],
        out_shape=[
            jax.ShapeDtypeStruct((s, D_MODEL), F32),
            jax.ShapeDtypeStruct((s, D_MODEL), BF16),
            jax.ShapeDtypeStruct((s, D_FF), BF16),
            jax.ShapeDtypeStruct((s, D_FF), BF16),
        ],
        scratch_shapes=[pltpu.VMEM((tm, D_MODEL), F32)],
        compiler_params=_cp("arbitrary", "arbitrary"),
    )(x, g, wgt, wut, wd)


def _ffn_bwd(dyb, dy, x, g, n, gg, uu, wgt, wut, wd, name):
    s = x.shape[0]
    tm = min(512, s)
    ni = s // tm
    nj = D_FF // FFN_TF

    def body(dyb_ref, dy_ref, x_ref, g_ref, n_ref, gg_ref, uu_ref, wg_ref, wu_ref, wd_ref,
             dx_ref, dxb_ref, dwg_ref, dwu_ref, dwd_ref, dg_ref, dn_ref):
        j = pl.program_id(0)
        i = pl.program_id(1)
        rows = pl.ds(pl.multiple_of(i * tm, tm), tm)
        dyv = dyb_ref[...]
        da = 0.5 * _nt(dyv, wd_ref[...])
        gv = gg_ref[...].astype(F32)
        uv = uu_ref[...].astype(F32)
        sg = _sigmoid(gv)
        si = gv * sg
        dgg = (da * uv * (sg * (1.0 + gv * (1.0 - sg)))).astype(BF16)
        duu = (da * si).astype(BF16)
        act = (si * uv).astype(BF16)
        nv = n_ref[...]
        first_i = i == 0
        _acc(dwd_ref, 0.5 * _tn(act, dyv), first_i)
        _acc(dwg_ref, _tn(dgg, nv), first_i)
        _acc(dwu_ref, _tn(duu, nv), first_i)
        cn = _nn(dgg, wg_ref[...]) + _nn(duu, wu_ref[...])

        @pl.when(j == 0)
        def _():
            dn_ref[rows, :] = cn

        @pl.when(j > 0)
        def _():
            dn_ref[rows, :] += cn

        @pl.when(j == nj - 1)
        def _():
            dn = dn_ref[rows, :]
            xv = x_ref[...]
            r = _rms_scale(xv, D_MODEL)
            dx = dy_ref[...] + _rms_bwd(dn, xv, g_ref[...], r, D_MODEL)
            dx_ref[...] = dx
            dxb_ref[...] = dx.astype(BF16)
            _acc(dg_ref, jnp.sum(dn * xv * r, axis=0, keepdims=True), first_i)

    last = lambda j, i: (jnp.where(j == nj - 1, i, 0), 0)
    row = lambda j, i: (i, 0)
    wrow = lambda j, i: (j, 0)
    return pl.pallas_call(
        body,
        name=name,
        grid=(nj, ni),
        in_specs=[
            pl.BlockSpec((tm, D_MODEL), row),
            pl.BlockSpec((tm, D_MODEL), last),
            pl.BlockSpec((tm, D_MODEL), last),
            pl.BlockSpec((1, D_MODEL), lambda j, i: (0, 0)),
            pl.BlockSpec((tm, D_MODEL), row),
            pl.BlockSpec((tm, FFN_TF), lambda j, i: (i, j)),
            pl.BlockSpec((tm, FFN_TF), lambda j, i: (i, j)),
            pl.BlockSpec((FFN_TF, D_MODEL), wrow),
            pl.BlockSpec((FFN_TF, D_MODEL), wrow),
            pl.BlockSpec((FFN_TF, D_MODEL), wrow),
        ],
        out_specs=[
            pl.BlockSpec((tm, D_MODEL), last),
            pl.BlockSpec((tm, D_MODEL), last),
            pl.BlockSpec((FFN_TF, D_MODEL), wrow),
            pl.BlockSpec((FFN_TF, D_MODEL), wrow),
            pl.BlockSpec((FFN_TF, D_MODEL), wrow),
            pl.BlockSpec((1, D_MODEL), lambda j, i: (0, 0)),
        ],
        out_shape=[
            jax.ShapeDtypeStruct((s, D_MODEL), F32),
            jax.ShapeDtypeStruct((s, D_MODEL), BF16),
            jax.ShapeDtypeStruct((D_FF, D_MODEL), F32),
            jax.ShapeDtypeStruct((D_FF, D_MODEL), F32),
            jax.ShapeDtypeStruct((D_FF, D_MODEL), F32),
            jax.ShapeDtypeStruct((1, D_MODEL), F32),
        ],
        scratch_shapes=[pltpu.VMEM((s, D_MODEL), F32)],
        compiler_params=_cp("arbitrary", "arbitrary"),
    )(dyb, dy, x, g, n, gg, uu, wgt, wut, wd)


def _heads_spec(tm):
    return pl.BlockSpec((N_HEADS, tm, HEAD_DIM), lambda i: (0, i, 0))


def _in_fwd(x1, gm, wint):
    s = x1.shape[0]
    tm = min(512, s)

    def body(x_ref, g_ref, w_ref, h_ref, xr_ref, gate_ref, q_ref, k_ref, v_ref):
        xv = x_ref[...]
        h = (xv * _rms_scale(xv, D_MODEL) * g_ref[...]).astype(BF16)
        h_ref[...] = h
        xr_ref[...] = _nt(h, w_ref[0:D_RNN, :])
        gate_ref[...] = _nt(h, w_ref[D_RNN:2 * D_RNN, :])
        for c, ref in ((2, q_ref), (3, k_ref), (4, v_ref)):
            p = _nt(h, w_ref[512 * c:512 * c + 512, :])
            for hh in range(N_HEADS):
                ref[hh] = p[:, HEAD_DIM * hh:HEAD_DIM * (hh + 1)]

    row = lambda i: (i, 0)
    hs = jax.ShapeDtypeStruct((N_HEADS, s, HEAD_DIM), F32)
    return pl.pallas_call(
        body,
        name="mix_in_fwd",
        grid=(s // tm,),
        in_specs=[
            pl.BlockSpec((tm, D_MODEL), row),
            pl.BlockSpec((1, D_MODEL), lambda i: (0, 0)),
            pl.BlockSpec((N_IN, D_MODEL), lambda i: (0, 0)),
        ],
        out_specs=[
            pl.BlockSpec((tm, D_MODEL), row),
            pl.BlockSpec((tm, D_RNN), row),
            pl.BlockSpec((tm, D_RNN), row),
            _heads_spec(tm), _heads_spec(tm), _heads_spec(tm),
        ],
        out_shape=[
            jax.ShapeDtypeStruct((s, D_MODEL), BF16),
            jax.ShapeDtypeStruct((s, D_RNN), F32),
            jax.ShapeDtypeStruct((s, D_RNN), F32),
            hs, hs, hs,
        ],
        compiler_params=_cp("arbitrary"),
    )(x1, gm, wint)


def _in_bwd(dx2, x1, gm, h, wint, dxr, dgate, dq, dk, dv):
    s = x1.shape[0]
    tm = min(256, s)

    def body(dx2_ref, x_ref, g_ref, h_ref, w_ref, dxr_ref, dgate_ref, dq_ref, dk_ref, dv_ref,
             dx1_ref, dx1b_ref, dw_ref, dg_ref):
        first = pl.program_id(0) == 0
        hv = h_ref[...]
        heads = lambda ref: jnp.concatenate([ref[hh] for hh in range(N_HEADS)], axis=1)
        pieces = (dxr_ref[...], dgate_ref[...], heads(dq_ref), heads(dk_ref), heads(dv_ref))
        dh = jnp.zeros((tm, D_MODEL), F32)
        for c, p in enumerate(pieces):
            pb = p.astype(BF16)
            rows = slice(512 * c, 512 * c + 512)
            dh = dh + _nn(pb, w_ref[rows, :])
            contrib = _tn(pb, hv)

            @pl.when(first)
            def _():
                dw_ref[rows, :] = contrib

            @pl.when(jnp.logical_not(first))
            def _():
                dw_ref[rows, :] += contrib

        xv = x_ref[...]
        r = _rms_scale(xv, D_MODEL)
        dx = dx2_ref[...] + _rms_bwd(dh, xv, g_ref[...], r, D_MODEL)
        dx1_ref[...] = dx
        dx1b_ref[...] = dx.astype(BF16)
        _acc(dg_ref, jnp.sum(dh * xv * r, axis=0, keepdims=True), first)

    row = lambda i: (i, 0)
    const = lambda i: (0, 0)
    return pl.pallas_call(
        body,
        name="mix_in_bwd",
        grid=(s // tm,),
        in_specs=[
            pl.BlockSpec((tm, D_MODEL), row),
            pl.BlockSpec((tm, D_MODEL), row),
            pl.BlockSpec((1, D_MODEL), const),
            pl.BlockSpec((tm, D_MODEL), row),
            pl.BlockSpec((N_IN, D_MODEL), const),
            pl.BlockSpec((tm, D_RNN), row),
            pl.BlockSpec((tm, D_RNN), row),
            _heads_spec(tm), _heads_spec(tm), _heads_spec(tm),
        ],
        out_specs=[
            pl.BlockSpec((tm, D_MODEL), row),
            pl.BlockSpec((tm, D_MODEL), row),
            pl.BlockSpec((N_IN, D_MODEL), const),
            pl.BlockSpec((1, D_MODEL), const),
        ],
        out_shape=[
            jax.ShapeDtypeStruct((s, D_MODEL), F32),
            jax.ShapeDtypeStruct((s, D_MODEL), BF16),
            jax.ShapeDtypeStruct((N_IN, D_MODEL), F32),
            jax.ShapeDtypeStruct((1, D_MODEL), F32),
        ],
        compiler_params=_cp("arbitrary"),
    )(dx2, x1, gm, h, wint, dxr, dgate, dq, dk, dv)


RG_CHUNK = 512
HALO = SUBLANES


def _rg_gates(xc, wa_ref, ba_ref, wx_ref, bx_ref, lam_ref):
    xcb = xc.astype(BF16)
    r = _sigmoid(_nn(xcb, wa_ref[...]) + ba_ref[...])
    ig = _sigmoid(_nn(xcb, wx_ref[...]) + bx_ref[...])
    ls = _log_sigmoid(lam_ref[...])
    log_a = RG_C * r * ls
    a = jnp.exp(log_a)
    em = _neg_expm1(2.0 * log_a)
    return xcb, r, ig, ls, a, em


def _conv(ext_ref, cw_ref, cb_ref, tc):
    xc = cb_ref[...] + cw_ref[0:1, :] * ext_ref[pl.ds(HALO - 3, tc), :]
    for j in range(1, CONV_W):
        xc = xc + cw_ref[j:j + 1, :] * ext_ref[pl.ds(HALO - 3 + j, tc), :]
    return xc


def _rg_fwd(xr, gate, cw, cb, wa, ba, wx, bx, lam, gro):
    s = xr.shape[0]
    tc = min(RG_CHUNK, s)
    nsub = tc // SUBLANES

    def body(xr_ref, gate_ref, cw_ref, cb_ref, wa_ref, ba_ref, wx_ref, bx_ref, lam_ref, gro_ref,
             h_ref, yn_ref, ext_ref, a_ref, b_ref, hc_ref):
        @pl.when(pl.program_id(0) == 0)
        def _():
            ext_ref[0:HALO, :] = jnp.zeros((HALO, D_RNN), F32)
            hc_ref[...] = jnp.zeros_like(hc_ref)

        ext_ref[HALO:HALO + tc, :] = xr_ref[...]
        xc = _conv(ext_ref, cw_ref, cb_ref, tc)
        _, _, ig, _, a, em = _rg_gates(xc, wa_ref, ba_ref, wx_ref, bx_ref, lam_ref)
        a_ref[...] = a
        b_ref[...] = jnp.sqrt(em) * (ig * xc)
        row = lax.broadcasted_iota(jnp.int32, (SUBLANES, D_RNN), 0)

        def step(k, hprev):
            rows = pl.ds(pl.multiple_of(k * SUBLANES, SUBLANES), SUBLANES)
            av = a_ref[rows, :]
            bv = b_ref[rows, :]
            for d in (1, 2, 4):
                a_sh = jnp.where(row >= d, pltpu.roll(av, d, 0), 1.0)
                b_sh = jnp.where(row >= d, pltpu.roll(bv, d, 0), 0.0)
                bv = av * b_sh + bv
                av = av * a_sh
            hv = av * hprev + bv
            h_ref[rows, :] = hv
            return hv[SUBLANES - 1:SUBLANES, :]

        hlast = lax.fori_loop(0, nsub, step, hc_ref[0:1, :])
        hc_ref[0:1, :] = hlast
        ext_ref[0:HALO, :] = xr_ref[tc - HALO:tc, :]
        y = h_ref[...] * _gelu(gate_ref[...])
        yn_ref[...] = (y * _rms_scale(y, D_RNN) * gro_ref[...]).astype(BF16)

    row = lambda c: (c, 0)
    const = lambda c: (0, 0)
    vec = pl.BlockSpec((1, D_RNN), const)
    mat = pl.BlockSpec((D_RNN, D_RNN), const)
    return pl.pallas_call(
        body,
        name="rglru_fwd",
        grid=(s // tc,),
        in_specs=[
            pl.BlockSpec((tc, D_RNN), row), pl.BlockSpec((tc, D_RNN), row),
            pl.BlockSpec((CONV_W, D_RNN), const), vec, mat, vec, mat, vec, vec, vec,
        ],
        out_specs=[pl.BlockSpec((tc, D_RNN), row), pl.BlockSpec((tc, D_RNN), row)],
        out_shape=[jax.ShapeDtypeStruct((s, D_RNN), F32), jax.ShapeDtypeStruct((s, D_RNN), BF16)],
        scratch_shapes=[
            pltpu.VMEM((tc + HALO, D_RNN), F32),
            pltpu.VMEM((tc, D_RNN), F32),
            pltpu.VMEM((tc, D_RNN), F32),
            pltpu.VMEM((SUBLANES, D_RNN), F32),
        ],
        compiler_params=_cp("arbitrary"),
    )(xr, gate, cw, cb, wa, ba, wx, bx, lam, gro)


def _rg_bwd(xr, gate, h, dyn, cw, cb, wa, ba, wx, bx, lam, gro):
    s = xr.shape[0]
    tc = min(RG_CHUNK, s)
    nt = s // tc
    nsub = tc // SUBLANES
    per = tc // HALO

    def body(xr_ref, xrh_ref, gate_ref, h_ref, hh_ref, dyn_ref,
             cw_ref, cb_ref, wa_ref, ba_ref, wx_ref, bx_ref, lam_ref, gro_ref,
             dxr_ref, dgate_ref, dcw_ref, dcb_ref, dwa_ref, dba_ref, dwx_ref, dbx_ref, dlam_ref, dgro_ref,
             extx_ref, exth_ref, exta_ref, extd_ref, cs_ref, dh_ref, g_ref, gc_ref, an_ref, dn_ref):
        c = pl.program_id(0)
        first = c == 0
        is_start = c == nt - 1

        @pl.when(first)
        def _():
            gc_ref[...] = jnp.zeros_like(gc_ref)
            an_ref[...] = jnp.zeros_like(an_ref)
            dn_ref[...] = jnp.zeros_like(dn_ref)

        zero_halo = jnp.zeros((HALO, D_RNN), F32)
        extx_ref[0:HALO, :] = jnp.where(is_start, zero_halo, xrh_ref[...])
        extx_ref[HALO:HALO + tc, :] = xr_ref[...]
        exth_ref[0:HALO, :] = jnp.where(is_start, zero_halo, hh_ref[...])
        exth_ref[HALO:HALO + tc, :] = h_ref[...]

        xc = _conv(extx_ref, cw_ref, cb_ref, tc)
        xcb, r, ig, ls, a, em = _rg_gates(xc, wa_ref, ba_ref, wx_ref, bx_ref, lam_ref)
        mult = jnp.sqrt(em)

        hv = h_ref[...]
        gt = gate_ref[...]
        ge = _gelu(gt)
        y = hv * ge
        rr = _rms_scale(y, D_RNN)
        dynv = dyn_ref[...]
        dy = _rms_bwd(dynv, y, gro_ref[...], rr, D_RNN)
        _acc(dgro_ref, jnp.sum(dynv * y * rr, axis=0, keepdims=True), first)
        dgate_ref[...] = dy * hv * _gelu_grad(gt)
        dh_ref[...] = dy * ge

        exta_ref[0:tc, :] = a
        exta_ref[tc:tc + HALO, :] = an_ref[...]
        an_ref[...] = a[0:HALO, :]
        cs_ref[...] = exta_ref[pl.ds(1, tc), :]
        row = lax.broadcasted_iota(jnp.int32, (SUBLANES, D_RNN), 0)

        def step(k, gnext):
            kk = nsub - 1 - k
            rows = pl.ds(pl.multiple_of(kk * SUBLANES, SUBLANES), SUBLANES)
            cv = cs_ref[rows, :]
            yv = dh_ref[rows, :]
            for d in (1, 2, 4):
                up = SUBLANES - d
                c_sh = jnp.where(row < up, pltpu.roll(cv, up, 0), 1.0)
                y_sh = jnp.where(row < up, pltpu.roll(yv, up, 0), 0.0)
                yv = yv + cv * y_sh
                cv = cv * c_sh
            gv = yv + cv * gnext
            g_ref[rows, :] = gv
            return gv[0:1, :]

        gfirst = lax.fori_loop(0, nsub, step, gc_ref[0:1, :])
        gc_ref[0:1, :] = gfirst

        gsc = g_ref[...]
        hprev = exth_ref[pl.ds(HALO - 1, tc), :]
        da = gsc * hprev
        dib = gsc * mult
        dmult = gsc * (ig * xc)
        dlog_a = da * a - dmult * ((1.0 - em) / mult)
        dr = dlog_a * (RG_C * ls)
        dls = jnp.sum(dlog_a * (RG_C * r), axis=0, keepdims=True)
        _acc(dlam_ref, dls * _sigmoid(-lam_ref[...]), first)
        dpa = dr * r * (1.0 - r)
        dpx = (dib * xc) * ig * (1.0 - ig)
        dpab = dpa.astype(BF16)
        dpxb = dpx.astype(BF16)
        dxc = dib * ig + _nt(dpab, wa_ref[...]) + _nt(dpxb, wx_ref[...])
        _acc(dwa_ref, _tn(xcb, dpab), first)
        _acc(dwx_ref, _tn(xcb, dpxb), first)
        _acc(dba_ref, jnp.sum(dpa, axis=0, keepdims=True), first)
        _acc(dbx_ref, jnp.sum(dpx, axis=0, keepdims=True), first)
        _acc(dcb_ref, jnp.sum(dxc, axis=0, keepdims=True), first)

        extd_ref[0:tc, :] = dxc
        extd_ref[tc:tc + HALO, :] = dn_ref[...]
        dn_ref[...] = dxc[0:HALO, :]
        dxr = cw_ref[0:1, :] * extd_ref[pl.ds(3, tc), :]
        for j in range(1, CONV_W):
            dxr = dxr + cw_ref[j:j + 1, :] * extd_ref[pl.ds(3 - j, tc), :]
        dxr_ref[...] = dxr
        dcw = jnp.concatenate(
            [jnp.sum(dxc * extx_ref[pl.ds(HALO - 3 + j, tc), :], axis=0, keepdims=True) for j in range(CONV_W)],
            axis=0)
        _acc(dcw_ref, dcw, first)

    rev = lambda c: (nt - 1 - c, 0)
    halo = lambda c: (jnp.maximum((nt - 1 - c) * per - 1, 0), 0)
    const = lambda c: (0, 0)
    vec = pl.BlockSpec((1, D_RNN), const)
    mat = pl.BlockSpec((D_RNN, D_RNN), const)
    chunk = pl.BlockSpec((tc, D_RNN), rev)
    vs = jax.ShapeDtypeStruct((1, D_RNN), F32)
    ms = jax.ShapeDtypeStruct((D_RNN, D_RNN), F32)
    return pl.pallas_call(
        body,
        name="rglru_bwd",
        grid=(nt,),
        in_specs=[
            chunk, pl.BlockSpec((HALO, D_RNN), halo), chunk, chunk, pl.BlockSpec((HALO, D_RNN), halo), chunk,
            pl.BlockSpec((CONV_W, D_RNN), const), vec, mat, vec, mat, vec, vec, vec,
        ],
        out_specs=[chunk, chunk, pl.BlockSpec((CONV_W, D_RNN), const), vec, mat, vec, mat, vec, vec, vec],
        out_shape=[
            jax.ShapeDtypeStruct((s, D_RNN), F32), jax.ShapeDtypeStruct((s, D_RNN), F32),
            jax.ShapeDtypeStruct((CONV_W, D_RNN), F32), vs, ms, vs, ms, vs, vs, vs,
        ],
        scratch_shapes=[
            pltpu.VMEM((tc + HALO, D_RNN), F32),
            pltpu.VMEM((tc + HALO, D_RNN), F32),
            pltpu.VMEM((tc + HALO, D_RNN), F32),
            pltpu.VMEM((tc + HALO, D_RNN), F32),
            pltpu.VMEM((tc, D_RNN), F32),
            pltpu.VMEM((tc, D_RNN), F32),
            pltpu.VMEM((tc, D_RNN), F32),
            pltpu.VMEM((SUBLANES, D_RNN), F32),
            pltpu.VMEM((HALO, D_RNN), F32),
            pltpu.VMEM((HALO, D_RNN), F32),
        ],
        compiler_params=_cp("arbitrary"),
    )(xr, xr, gate, h, h, dyn, cw, cb, wa, ba, wx, bx, lam, gro)


ATT_BLOCK = 512
ATT_SCALE = 1.0 / math.sqrt(HEAD_DIM)


def _split_dot(v, m):
    hi = v.astype(BF16)
    lo = (v - hi.astype(F32)).astype(BF16)
    return _nn(hi, m) + _nn(lo, m)


def _log_terms(z):
    lb = jnp.minimum(z, 0.0) - jnp.log(1.0 + jnp.exp(-jnp.abs(z)))
    return lb, lb - z


def _tri(n, pred):
    r = lax.broadcasted_iota(jnp.int32, (n, n), 0)
    c = lax.broadcasted_iota(jnp.int32, (n, n), 1)
    return pred(r, c).astype(BF16)


def _causal(n):
    return lax.broadcasted_iota(jnp.int32, (n, n), 1) < lax.broadcasted_iota(jnp.int32, (n, n), 0)


def _attn_fwd(q, k, v, gq, gk):
    s = q.shape[1]
    blk = min(ATT_BLOCK, s)
    sub = blk // 2

    def body(q_ref, k_ref, v_ref, gq_ref, gk_ref, ol_ref, kn_ref, vb_ref):
        qi = pl.program_id(1)

        @pl.when(qi == 0)
        def _():
            kv = k_ref[0]
            kn_ref[...] = (kv * _rms_scale(kv, HEAD_DIM) * gk_ref[...]).astype(BF16)
            vb_ref[...] = v_ref[0].astype(BF16)

        qv = q_ref[0]
        qn = ((qv * _rms_scale(qv, HEAD_DIM) * gq_ref[...]) * ATT_SCALE).astype(BF16)
        after = _tri(sub, lambda r, c: r > c)

        def block(kj, carry, acc, diag):
            rows = pl.ds(pl.multiple_of(kj * blk, blk), blk)
            z = _nt(qn, kn_ref[rows, :])
            lb, l1 = _log_terms(z)
            if diag:
                causal = _causal(blk)
                l1 = jnp.where(causal, l1, 0.0)
            l1a, l1b = l1[:, 0:sub], l1[:, sub:blk]
            csa = _split_dot(l1a, after)
            csb = _split_dot(l1b, after)
            carry_a = carry + csb[:, 0:1] + l1b[:, 0:1]
            w = jnp.exp(lb + jnp.concatenate([csa + carry_a, csb + carry], axis=1))
            if diag:
                w = jnp.where(causal, w, 0.0)
            acc = acc + _nn(w.astype(BF16), vb_ref[rows, :])
            return carry_a + csa[:, 0:1] + l1a[:, 0:1], acc

        carry, acc = block(qi, jnp.zeros((blk, 1), F32), jnp.zeros((blk, HEAD_DIM), F32), True)
        carry, acc = lax.fori_loop(
            0, qi, lambda t, ca: block(qi - 1 - t, ca[0], ca[1], False), (carry, acc))
        ol_ref[0] = jnp.concatenate([acc, jnp.broadcast_to(carry, (blk, HEAD_DIM))], axis=1)

    head = pl.BlockSpec((1, s, HEAD_DIM), lambda hh, i: (hh, 0, 0))
    gain = pl.BlockSpec((1, HEAD_DIM), lambda hh, i: (0, 0))
    return pl.pallas_call(
        body,
        name="attn_fwd",
        grid=(N_HEADS, s // blk),
        in_specs=[pl.BlockSpec((1, blk, HEAD_DIM), lambda hh, i: (hh, i, 0)), head, head, gain, gain],
        out_specs=pl.BlockSpec((1, blk, LANES), lambda hh, i: (hh, i, 0)),
        out_shape=jax.ShapeDtypeStruct((N_HEADS, s, LANES), F32),
        scratch_shapes=[pltpu.VMEM((s, HEAD_DIM), BF16), pltpu.VMEM((s, HEAD_DIM), BF16)],
        compiler_params=_cp("arbitrary", "arbitrary"),
    )(q, k, v, gq, gk)


def _attn_bwd(q, k, v, gq, gk, ol, do):
    s = q.shape[1]
    blk = min(ATT_BLOCK, s)
    sub = blk // 2
    ni = s // blk

    def body(q_ref, k_ref, v_ref, gq_ref, gk_ref, ol_ref, do_ref,
             dq_ref, dk_ref, dv_ref, dgq_ref, dgk_ref, kn_ref, vb_ref, dkn_ref):
        hh = pl.program_id(0)
        qi = pl.program_id(1)

        @pl.when(qi == 0)
        def _():
            kv = k_ref[0]
            kn_ref[...] = (kv * _rms_scale(kv, HEAD_DIM) * gk_ref[...]).astype(BF16)
            vb_ref[...] = v_ref[0].astype(BF16)
            dkn_ref[...] = jnp.zeros_like(dkn_ref)
            dv_ref[...] = jnp.zeros_like(dv_ref)

        qv = q_ref[0]
        rq = _rms_scale(qv, HEAD_DIM)
        qn = ((qv * rq * gq_ref[...]) * ATT_SCALE).astype(BF16)
        dov = do_ref[0]
        dob = dov.astype(BF16)
        ltot = ol_ref[0][:, HEAD_DIM:HEAD_DIM + 1]
        before = _tri(sub, lambda r, c: r < c)
        last = slice(sub - 1, sub)

        def block(kj, cl, ce, dqn, diag):
            rows = pl.ds(pl.multiple_of(kj * blk, blk), blk)
            kb = kn_ref[rows, :]
            z = _nt(qn, kb)
            lb, l1 = _log_terms(z)
            if diag:
                causal = _causal(blk)
                l1 = jnp.where(causal, l1, 0.0)
            l1a, l1b = l1[:, 0:sub], l1[:, sub:blk]
            prea = _split_dot(l1a, before)
            preb = _split_dot(l1b, before)
            rest_a = ltot - cl
            sum_a = prea[:, last] + l1a[:, last]
            rest_b = rest_a - sum_a
            tail = jnp.concatenate([rest_a - prea - l1a, rest_b - preb - l1b], axis=1)
            w = jnp.exp(lb + tail)
            if diag:
                w = jnp.where(causal, w, 0.0)
            dv_ref[0, rows, :] += _tn(w.astype(BF16), dob)
            e = w * _nt(dob, vb_ref[rows, :])
            eb = e.astype(BF16)
            ea, ebb = eb[:, 0:sub], eb[:, sub:blk]
            epa = _nn(ea, before)
            epb = _nn(ebb, before)
            ce_b = ce + epa[:, last] + ea[:, last].astype(F32)
            esum = jnp.concatenate([ce + epa, ce_b + epb], axis=1)
            dz = e - jnp.exp(lb) * (e + esum)
            if diag:
                dz = jnp.where(causal, dz, 0.0)
            dzb = dz.astype(BF16)
            dqn = dqn + _nn(dzb, kb)
            dkn_ref[rows, :] += _tn(dzb, qn)
            cl = cl + sum_a + preb[:, last] + l1b[:, last]
            ce = ce_b + epb[:, last] + ebb[:, last].astype(F32)
            return cl, ce, dqn

        zero = jnp.zeros((blk, 1), F32)
        cl, ce, dqn = lax.fori_loop(
            0, qi, lambda t, c: block(t, c[0], c[1], c[2], False), (zero, zero, jnp.zeros((blk, HEAD_DIM), F32)))
        _, _, dqn = block(qi, cl, ce, dqn, True)

        first = jnp.logical_and(hh == 0, qi == 0)
        gq = gq_ref[...]
        dqs = dqn * ATT_SCALE
        dq_ref[0] = _rms_bwd(dqs, qv, gq, rq, HEAD_DIM)
        _acc(dgq_ref, jnp.sum(dqs * qv * rq, axis=0, keepdims=True), first)

        @pl.when(qi == ni - 1)
        def _():
            kv = k_ref[0]
            rk = _rms_scale(kv, HEAD_DIM)
            dkn = dkn_ref[...]
            dk_ref[0] = _rms_bwd(dkn, kv, gk_ref[...], rk, HEAD_DIM)
            _acc(dgk_ref, jnp.sum(dkn * kv * rk, axis=0, keepdims=True), hh == 0)

    head = pl.BlockSpec((1, s, HEAD_DIM), lambda hh, i: (hh, 0, 0))
    qblk = pl.BlockSpec((1, blk, HEAD_DIM), lambda hh, i: (hh, i, 0))
    gain = pl.BlockSpec((1, HEAD_DIM), lambda hh, i: (0, 0))
    hs = jax.ShapeDtypeStruct((N_HEADS, s, HEAD_DIM), F32)
    gs = jax.ShapeDtypeStruct((1, HEAD_DIM), F32)
    return pl.pallas_call(
        body,
        name="attn_bwd",
        grid=(N_HEADS, ni),
        in_specs=[qblk, head, head, gain, gain, pl.BlockSpec((1, blk, LANES), lambda hh, i: (hh, i, 0)), qblk],
        out_specs=[qblk, head, head, gain, gain],
        out_shape=[hs, hs, hs, gs, gs],
        scratch_shapes=[
            pltpu.VMEM((s, HEAD_DIM), BF16), pltpu.VMEM((s, HEAD_DIM), BF16), pltpu.VMEM((s, HEAD_DIM), F32)],
        compiler_params=_cp("arbitrary", "arbitrary"),
    )(q, k, v, gq, gk, ol, do)


def _att_norm(ol_ref, g_ref):
    ov = [ol_ref[hh][:, 0:HEAD_DIM] for hh in range(N_HEADS)]
    ss = ov[0] * ov[0]
    for hh in range(1, N_HEADS):
        ss = ss + ov[hh] * ov[hh]
    ra = lax.rsqrt(jnp.sum(ss, axis=-1, keepdims=True) * (1.0 / D_ATT) + EPS)
    ya = jnp.concatenate([ov[hh] * ra * g_ref[hh] for hh in range(N_HEADS)], axis=1).astype(BF16)
    return ov, ra, ya


def _out_fwd(x1, ynr, ol, gao, wout):
    s = x1.shape[0]
    tm = min(512, s)

    def body(x_ref, ynr_ref, ol_ref, g_ref, w_ref, x2_ref):
        _, _, ya = _att_norm(ol_ref, g_ref)
        x2_ref[...] = x_ref[...] + _nn(ynr_ref[...], w_ref[0:D_RNN, :]) + _nn(ya, w_ref[D_RNN:D_MODEL, :])

    row = lambda i: (i, 0)
    return pl.pallas_call(
        body,
        name="mix_out_fwd",
        grid=(s // tm,),
        in_specs=[
            pl.BlockSpec((tm, D_MODEL), row),
            pl.BlockSpec((tm, D_RNN), row),
            pl.BlockSpec((N_HEADS, tm, LANES), lambda i: (0, i, 0)),
            pl.BlockSpec((N_HEADS, 1, HEAD_DIM), lambda i: (0, 0, 0)),
            pl.BlockSpec((D_MODEL, D_MODEL), lambda i: (0, 0)),
        ],
        out_specs=pl.BlockSpec((tm, D_MODEL), row),
        out_shape=jax.ShapeDtypeStruct((s, D_MODEL), F32),
        compiler_params=_cp("arbitrary"),
    )(x1, ynr, ol, gao, wout)


def _out_bwd(dx2b, ynr, ol, gao, wout):
    s = ynr.shape[0]
    tm = min(512, s)

    def body(dx_ref, ynr_ref, ol_ref, g_ref, w_ref, dynr_ref, do_ref, dw_ref, dg_ref):
        first = pl.program_id(0) == 0
        dxb = dx_ref[...]
        dynr_ref[...] = _nt(dxb, w_ref[0:D_RNN, :])
        dya = _nt(dxb, w_ref[D_RNN:D_MODEL, :])
        ov, ra, ya = _att_norm(ol_ref, g_ref)
        c_r = _tn(ynr_ref[...], dxb)
        c_a = _tn(ya, dxb)

        @pl.when(first)
        def _():
            dw_ref[0:D_RNN, :] = c_r
            dw_ref[D_RNN:D_MODEL, :] = c_a

        @pl.when(jnp.logical_not(first))
        def _():
            dw_ref[0:D_RNN, :] += c_r
            dw_ref[D_RNN:D_MODEL, :] += c_a

        dyh = [dya[:, HEAD_DIM * hh:HEAD_DIM * (hh + 1)] for hh in range(N_HEADS)]
        uo = dyh[0] * g_ref[0] * ov[0]
        for hh in range(1, N_HEADS):
            uo = uo + dyh[hh] * g_ref[hh] * ov[hh]
        m = jnp.sum(uo, axis=-1, keepdims=True) * (1.0 / D_ATT)
        r3m = ra * ra * ra * m
        for hh in range(N_HEADS):
            do_ref[hh] = ra * (dyh[hh] * g_ref[hh]) - ov[hh] * r3m
            contrib = jnp.sum(dyh[hh] * ov[hh] * ra, axis=0, keepdims=True)

            @pl.when(first)
            def _():
                dg_ref[hh] = contrib

            @pl.when(jnp.logical_not(first))
            def _():
                dg_ref[hh] += contrib

    row = lambda i: (i, 0)
    return pl.pallas_call(
        body,
        name="mix_out_bwd",
        grid=(s // tm,),
        in_specs=[
            pl.BlockSpec((tm, D_MODEL), row),
            pl.BlockSpec((tm, D_RNN), row),
            pl.BlockSpec((N_HEADS, tm, LANES), lambda i: (0, i, 0)),
            pl.BlockSpec((N_HEADS, 1, HEAD_DIM), lambda i: (0, 0, 0)),
            pl.BlockSpec((D_MODEL, D_MODEL), lambda i: (0, 0)),
        ],
        out_specs=[
            pl.BlockSpec((tm, D_RNN), row),
            _heads_spec(tm),
            pl.BlockSpec((D_MODEL, D_MODEL), lambda i: (0, 0)),
            pl.BlockSpec((N_HEADS, 1, HEAD_DIM), lambda i: (0, 0, 0)),
        ],
        out_shape=[
            jax.ShapeDtypeStruct((s, D_RNN), F32),
            jax.ShapeDtypeStruct((N_HEADS, s, HEAD_DIM), F32),
            jax.ShapeDtypeStruct((D_MODEL, D_MODEL), F32),
            jax.ShapeDtypeStruct((N_HEADS, 1, HEAD_DIM), F32),
        ],
        compiler_params=_cp("arbitrary"),
    )(dx2b, ynr, ol, gao, wout)


def _loss_head(y, target):
    s = y.shape[0]
    tm = min(512, s)

    def body(y_ref, t_ref, dy_ref, dyb_ref, loss_ref):
        d = y_ref[...] - t_ref[...]
        dy = d * (1.0 / D_MODEL)
        dy_ref[...] = dy
        dyb_ref[...] = dy.astype(BF16)
        part = 0.5 * jnp.sum(jnp.sum(d * d, axis=-1, keepdims=True) * (1.0 / D_MODEL), axis=0, keepdims=True)
        _acc(loss_ref, jnp.broadcast_to(part, (SUBLANES, LANES)), pl.program_id(0) == 0)

    row = lambda i: (i, 0)
    return pl.pallas_call(
        body,
        name="loss_head",
        grid=(s // tm,),
        in_specs=[pl.BlockSpec((tm, D_MODEL), row), pl.BlockSpec((tm, D_MODEL), row)],
        out_specs=[
            pl.BlockSpec((tm, D_MODEL), row), pl.BlockSpec((tm, D_MODEL), row),
            pl.BlockSpec((SUBLANES, LANES), lambda i: (0, 0)),
        ],
        out_shape=[
            jax.ShapeDtypeStruct((s, D_MODEL), F32), jax.ShapeDtypeStruct((s, D_MODEL), BF16),
            jax.ShapeDtypeStruct((SUBLANES, LANES), F32),
        ],
        compiler_params=_cp("arbitrary"),
    )(y, target)


def _block_diag(w):
    out = jnp.zeros((D_RNN, D_RNN), w.dtype)
    for nblk in range(w.shape[0]):
        lo = nblk * 64
        out = lax.dynamic_update_slice(out, w[nblk], (lo, lo))
    return out


def _diag_blocks(m):
    return jnp.stack([m[64 * nblk:64 * (nblk + 1), 64 * nblk:64 * (nblk + 1)] for nblk in range(8)])


def _local_step(x, target, p):
    wa = _block_diag(p["rg_w_a"]).astype(BF16)
    wx = _block_diag(p["rg_w_x"]).astype(BF16)
    gao = p["attn_out_norm"].reshape(N_HEADS, 1, HEAD_DIM)
    rg_args = (p["conv_w"], p["conv_b"], wa, p["rg_b_a"], wx, p["rg_b_x"], p["rg_lambda"], p["rnn_out_norm"])

    x1, n1, gg1, uu1 = _ffn_fwd(x, p["ffn1_norm"], p["wg1t"], p["wu1t"], p["wd1"], "ffn1_fwd")
    hmix, xr, gate, q, k, v = _in_fwd(x1, p["mix_norm"], p["wint"])
    hrec, ynr = _rg_fwd(xr, gate, *rg_args)
    ol = _attn_fwd(q, k, v, p["q_norm"], p["k_norm"])
    x2 = _out_fwd(x1, ynr, ol, gao, p["wout"])
    x3, n2, gg2, uu2 = _ffn_fwd(x2, p["ffn2_norm"], p["wg2t"], p["wu2t"], p["wd2"], "ffn2_fwd")
    dy3, dy3b, loss_tile = _loss_head(x3, target)

    g = {}
    dx2, dx2b, g["wg2t"], g["wu2t"], g["wd2"], g["ffn2_norm"] = _ffn_bwd(
        dy3b, dy3, x2, p["ffn2_norm"], n2, gg2, uu2, p["wg2t"], p["wu2t"], p["wd2"], "ffn2_bwd")
    dynr, do, g["wout"], dgao = _out_bwd(dx2b, ynr, ol, gao, p["wout"])
    g["attn_out_norm"] = dgao.reshape(1, D_ATT)
    dq, dk, dv, g["q_norm"], g["k_norm"] = _attn_bwd(q, k, v, p["q_norm"], p["k_norm"], ol, do)
    (dxr, dgate, g["conv_w"], g["conv_b"], dwa, g["rg_b_a"], dwx, g["rg_b_x"], g["rg_lambda"],
     g["rnn_out_norm"]) = _rg_bwd(xr, gate, hrec, dynr, *rg_args)
    g["rg_w_a"] = _diag_blocks(dwa)
    g["rg_w_x"] = _diag_blocks(dwx)
    dx1, dx1b, g["wint"], g["mix_norm"] = _in_bwd(dx2, x1, p["mix_norm"], hmix, p["wint"], dxr, dgate, dq, dk, dv)
    dx0, _, g["wg1t"], g["wu1t"], g["wd1"], g["ffn1_norm"] = _ffn_bwd(
        dx1b, dx1, x, p["ffn1_norm"], n1, gg1, uu1, p["wg1t"], p["wu1t"], p["wd1"], "ffn1_bwd")
    return loss_tile[0, 0], dx0, g


ANY = pl.BlockSpec(memory_space=pl.ANY)


def _me():
    return 4 * lax.axis_index("x") + 2 * lax.axis_index("y") + lax.axis_index("c")


def _peer(k):
    x, y, c = lax.axis_index("x"), lax.axis_index("y"), lax.axis_index("c")
    px = 1 - x if k & 4 else x
    py = 1 - y if k & 2 else y
    pc = 1 - c if k & 1 else c
    return (px, py, pc), 4 * px + 2 * py + pc


def _all_gather_rows(blocks, name):
    n = len(blocks)

    def body(*refs):
        ins, outs = refs[:n], refs[n:2 * n]
        send, recv, loc = refs[2 * n:]
        me = _me()

        def rows(a, dev):
            r = ins[a].shape[0]
            return outs[a].at[pl.ds(pl.multiple_of(dev * r, SUBLANES), r), :]

        def remote(a, k, dev):
            peer, _ = _peer(k)
            return pltpu.make_async_remote_copy(
                src_ref=ins[a], dst_ref=rows(a, dev), send_sem=send.at[a, k - 1], recv_sem=recv.at[a, k - 1],
                device_id=peer, device_id_type=MESH)

        local = [pltpu.make_async_copy(ins[a], rows(a, me), loc.at[a]) for a in range(n)]
        for a in range(n):
            local[a].start()
            for k in range(1, N_DEV):
                remote(a, k, me).start()
        for a in range(n):
            for k in range(1, N_DEV):
                remote(a, k, _peer(k)[1]).wait()
            local[a].wait()

    return pl.pallas_call(
        body,
        name=name,
        in_specs=[ANY] * n,
        out_specs=[ANY] * n,
        out_shape=[jax.ShapeDtypeStruct((N_DEV * b.shape[0], b.shape[1]), b.dtype) for b in blocks],
        scratch_shapes=[
            pltpu.SemaphoreType.DMA((n, N_DEV - 1)),
            pltpu.SemaphoreType.DMA((n, N_DEV - 1)),
            pltpu.SemaphoreType.DMA((n,)),
        ],
    )(*blocks)


def _exchange_partials(parts, name):
    n = len(parts)

    def body(*refs):
        ins, outs = refs[:n], refs[n:2 * n]
        send, recv, loc = refs[2 * n:]
        me = _me()

        def rows(a, dev):
            r = ins[a].shape[0] // N_DEV
            return ins[a].at[pl.ds(pl.multiple_of(dev * r, SUBLANES), r), :]

        def remote(a, k, slot):
            peer, pid = _peer(k)
            return pltpu.make_async_remote_copy(
                src_ref=rows(a, pid), dst_ref=outs[a].at[slot], send_sem=send.at[a, k - 1], recv_sem=recv.at[a, k - 1],
                device_id=peer, device_id_type=MESH)

        local = [pltpu.make_async_copy(rows(a, me), outs[a].at[me], loc.at[a]) for a in range(n)]
        for a in range(n):
            local[a].start()
            for k in range(1, N_DEV):
                remote(a, k, me).start()
        for a in range(n):
            for k in range(1, N_DEV):
                remote(a, k, _peer(k)[1]).wait()
            local[a].wait()

    return pl.pallas_call(
        body,
        name=name,
        in_specs=[ANY] * n,
        out_specs=[ANY] * n,
        out_shape=[jax.ShapeDtypeStruct((N_DEV, b.shape[0] // N_DEV, b.shape[1]), b.dtype) for b in parts],
        scratch_shapes=[
            pltpu.SemaphoreType.DMA((n, N_DEV - 1)),
            pltpu.SemaphoreType.DMA((n, N_DEV - 1)),
            pltpu.SemaphoreType.DMA((n,)),
        ],
    )(*parts)


def _row_tile(rows, cap):
    best = SUBLANES
    for t in range(SUBLANES, min(rows, cap) + 1, SUBLANES):
        if rows % t == 0:
            best = t
    return best


def _sum_slots(land, name):
    _, r, c = land.shape
    tm = _row_tile(r, 256)

    def body(l_ref, o_ref):
        acc = l_ref[0]
        for d in range(1, N_DEV):
            acc = acc + l_ref[d]
        o_ref[...] = acc

    return pl.pallas_call(
        body,
        name=name,
        grid=(r // tm,),
        in_specs=[pl.BlockSpec((N_DEV, tm, c), lambda i: (0, i, 0))],
        out_specs=pl.BlockSpec((tm, c), lambda i: (i, 0)),
        out_shape=jax.ShapeDtypeStruct((r, c), F32),
        compiler_params=_cp("arbitrary"),
    )(land)


def _adamw(w, g, m, v, name):
    r, c = w.shape
    tm = _row_tile(r, 512)

    def body(w_ref, g_ref, m_ref, v_ref, d_ref, nm_ref, nv_ref):
        gv = g_ref[...]
        nm = ADAM_B1 * m_ref[...] + (1.0 - ADAM_B1) * gv
        nv = ADAM_B2 * v_ref[...] + (1.0 - ADAM_B2) * (gv * gv)
        m_hat = nm / (1.0 - ADAM_B1 ** ADAM_STEP)
        v_hat = nv / (1.0 - ADAM_B2 ** ADAM_STEP)
        d_ref[...] = -ADAM_LR * (m_hat / (jnp.sqrt(v_hat) + ADAM_EPS) + ADAM_WD * w_ref[...])
        nm_ref[...] = nm
        nv_ref[...] = nv

    spec = pl.BlockSpec((tm, c), lambda i: (i, 0))
    shape = jax.ShapeDtypeStruct((r, c), F32)
    return pl.pallas_call(
        body,
        name=name,
        grid=(r // tm,),
        in_specs=[spec] * 4,
        out_specs=[spec] * 3,
        out_shape=[shape] * 3,
        compiler_params=_cp("arbitrary"),
    )(w, g, m, v)


SMALL = ("ffn1_norm", "mix_norm", "conv_w", "conv_b", "rg_w_a", "rg_b_a", "rg_w_x", "rg_b_x", "rg_lambda",
         "q_norm", "k_norm", "rnn_out_norm", "attn_out_norm", "ffn2_norm")
PACK_GRANULE = 64 * LANES


def _pack(arrs):
    flat = jnp.concatenate([a.reshape(-1) for a in arrs])
    pad = -flat.shape[0] % PACK_GRANULE
    return jnp.pad(flat, (0, pad)).reshape(-1, LANES)


def _unpack(buf, shapes):
    flat = buf.reshape(-1)
    out, off = [], 0
    for shp in shapes:
        size = math.prod(shp)
        out.append(flat[off:off + size].reshape(shp))
        off += size
    return out


def kernel(x, ffn1_norm, ffn1_w_gate, ffn1_w_up, ffn1_w_down, mix_norm, w_in, conv_w, conv_b, rg_w_a, rg_b_a, rg_w_x, rg_b_x, rg_lambda, q_norm, k_norm, rnn_out_norm, attn_out_norm, w_out, ffn2_norm, ffn2_w_gate, ffn2_w_up, ffn2_w_down, loss_target, m_ffn1_norm, m_ffn1_w_gate, m_ffn1_w_up, m_ffn1_w_down, m_mix_norm, m_w_in, m_conv_w, m_conv_b, m_rg_w_a, m_rg_b_a, m_rg_w_x, m_rg_b_x, m_rg_lambda, m_q_norm, m_k_norm, m_rnn_out_norm, m_attn_out_norm, m_w_out, m_ffn2_norm, m_ffn2_w_gate, m_ffn2_w_up, m_ffn2_w_down, v_ffn1_norm, v_ffn1_w_gate, v_ffn1_w_up, v_ffn1_w_down, v_mix_norm, v_w_in, v_conv_w, v_conv_b, v_rg_w_a, v_rg_b_a, v_rg_w_x, v_rg_b_x, v_rg_lambda, v_q_norm, v_k_norm, v_rnn_out_norm, v_attn_out_norm, v_w_out, v_ffn2_norm, v_ffn2_w_gate, v_ffn2_w_up, v_ffn2_w_down):
    given = dict(locals())
    names = ("ffn1_norm", "ffn1_w_gate", "ffn1_w_up", "ffn1_w_down", "mix_norm", "w_in", "conv_w", "conv_b",
             "rg_w_a", "rg_b_a", "rg_w_x", "rg_b_x", "rg_lambda", "q_norm", "k_norm", "rnn_out_norm",
             "attn_out_norm", "w_out", "ffn2_norm", "ffn2_w_gate", "ffn2_w_up", "ffn2_w_down")
    me = _me()

    transposed = {"ffn1_w_gate": "wg1t", "ffn1_w_up": "wu1t", "w_in": "wint", "ffn2_w_gate": "wg2t", "ffn2_w_up": "wu2t"}
    straight = {"ffn1_w_down": "wd1", "w_out": "wout", "ffn2_w_down": "wd2"}
    big = {**transposed, **straight}
    keys = list(big.values())
    shards = [given[n][0].T.astype(BF16) if n in transposed else given[n][0].astype(BF16) for n in big]
    cw_tile = jnp.zeros((SUBLANES, LANES), F32).at[0:CONV_W, 0:D_RNN // N_DEV].set(conv_w[0])
    gathered = _all_gather_rows(shards + [cw_tile], "gather_weights")
    p = dict(zip(keys, gathered[:-1]))
    cw_all = gathered[-1].reshape(N_DEV, SUBLANES, LANES)[:, 0:CONV_W, 0:D_RNN // N_DEV]
    p["conv_w"] = jnp.transpose(cw_all, (1, 0, 2)).reshape(CONV_W, D_RNN)
    p["rg_w_a"] = rg_w_a[0]
    p["rg_w_x"] = rg_w_x[0]
    for n in ("ffn1_norm", "mix_norm", "conv_b", "rg_b_a", "rg_b_x", "rg_lambda", "q_norm", "k_norm",
              "rnn_out_norm", "attn_out_norm", "ffn2_norm"):
        p[n] = given[n]

    loss_local, dx, g = _local_step(x[0], loss_target[0], p)
    loss = lax.psum(loss_local, ("x", "y", "c"))

    landed = _exchange_partials([g[k] for k in keys], "exchange_grads")
    grads, deltas, new_m, new_v = {}, {}, {}, {}
    for n, key, land in zip(big, keys, landed):
        gsum = _sum_slots(land, "sum_" + key)
        if n in transposed:
            gsum = gsum.T
        grads[n] = gsum[None]
        d, nm, nv = _adamw(given[n][0], gsum, given["m_" + n][0], given["v_" + n][0], "adamw_" + key)
        deltas[n], new_m[n], new_v[n] = d[None], nm[None], nv[None]

    small_shapes = [(1, CONV_W, D_RNN) if n == "conv_w" else given[n].shape for n in SMALL]
    packed = _pack([g[n] for n in SMALL])
    rows = packed.shape[0]
    all_small = _all_gather_rows([packed], "gather_small_grads")[0].reshape(N_DEV, rows, LANES)
    summed = dict(zip(SMALL, _unpack(_sum_slots(all_small, "sum_small"), small_shapes)))
    summed["conv_w"] = lax.dynamic_slice_in_dim(summed["conv_w"], me * (D_RNN // N_DEV), D_RNN // N_DEV, axis=2)
    local_shapes = [given[n].shape for n in SMALL]
    d, nm, nv = _adamw(_pack([given[n] for n in SMALL]), _pack([summed[n] for n in SMALL]),
                       _pack([given["m_" + n] for n in SMALL]), _pack([given["v_" + n] for n in SMALL]), "adamw_small")
    for n, dd, mm, vv in zip(SMALL, _unpack(d, local_shapes), _unpack(nm, local_shapes), _unpack(nv, local_shapes)):
        grads[n], deltas[n], new_m[n], new_v[n] = summed[n], dd, mm, vv

    return (loss, dx[None], *[grads[n] for n in names], *[deltas[n] for n in names],
            *[new_m[n] for n in names], *[new_v[n] for n in names])
```
